```python
import math
import jax, jax.numpy as jnp
from jax import lax
import numpy as np

D_MODEL = 1024
BATCH = 16
SEQ = 4096
DEPTH = 4

CHUNK = 64
Q_BLOCK = 128
N_MIXERS = 3
RMS_EPS = 1e-6
DA_HEADS = D_MODEL // 128
DA_HEAD_DIM = 64
DA_V_DIM = 2 * DA_HEAD_DIM
ROPE_THETA = 500000.0
ROPE_DIM = DA_HEAD_DIM // 4
CONV_KERNEL = 31
GDN_K_HEADS = D_MODEL // 128
GDN_V_HEADS = 2 * GDN_K_HEADS
GDN_HEAD_DIM = 128
GDN_CONV = 4
GDN_KEY_DIM = GDN_K_HEADS * GDN_HEAD_DIM
GDN_VAL_DIM = GDN_V_HEADS * GDN_HEAD_DIM
GDN_IN_DIM = 2 * GDN_KEY_DIM + 2 * GDN_VAL_DIM + 2 * GDN_V_HEADS
D_FF = 256 * ((8 * D_MODEL // 3 + 255) // 256)
FFN_CONV = 3
N_ATTN = (DEPTH + 2) // 3
N_CONVM = (DEPTH + 1) // 3
N_GDN = DEPTH // 3

kernel_name = "hybrid_diffattn_conformer_gdn_trunk"


def rmsnorm(x, g, eps=RMS_EPS):
    xf = x.astype(jnp.float32)
    y = xf * lax.rsqrt(jnp.mean(xf * xf, axis=-1, keepdims=True) + eps)
    return (y * g.astype(jnp.float32)).astype(x.dtype)


def causal_dwconv(x, w):
    K, C = w.shape
    return lax.conv_general_dilated(x, w[:, None, :].astype(x.dtype), window_strides=(1,),
                                    padding=[(K - 1, 0)], dimension_numbers=("NWC", "WIO", "NWC"),
                                    feature_group_count=C)


def partial_rope(x, positions):
    half = ROPE_DIM // 2
    inv_freq = jnp.power(ROPE_THETA, -jnp.arange(half, dtype=jnp.float32) * 2.0 / ROPE_DIM)
    ang = positions.astype(jnp.float32)[:, :, None] * inv_freq
    cos = jnp.cos(ang)[:, :, None, None, :].astype(x.dtype)
    sin = jnp.sin(ang)[:, :, None, None, :].astype(x.dtype)
    x1 = x[..., :half]
    x2 = x[..., half:ROPE_DIM]
    return jnp.concatenate([x1 * cos - x2 * sin, x2 * cos + x1 * sin, x[..., ROPE_DIM:]], axis=-1)


def lambda_init_fn(layer):
    return 0.8 - 0.6 * math.exp(-0.3 * layer)


def diff_attention(h, positions, w_qkv, lam_p, subln, w_o, lambda_init):
    B, S, _ = h.shape
    qkv = h @ w_qkv
    nq = DA_HEADS * 2 * DA_HEAD_DIM
    q = qkv[..., :nq].reshape(B, S, DA_HEADS, 2, DA_HEAD_DIM)
    k = qkv[..., nq:2 * nq].reshape(B, S, DA_HEADS, 2, DA_HEAD_DIM)
    v = qkv[..., 2 * nq:].reshape(B, S, DA_HEADS, DA_V_DIM)
    q = partial_rope(q, positions) * (DA_HEAD_DIM ** -0.5)
    k = partial_rope(k, positions)
    lp = lam_p.astype(jnp.float32)
    lam = jnp.exp(jnp.sum(lp[0] * lp[1])) - jnp.exp(jnp.sum(lp[2] * lp[3])) + lambda_init
    cid = np.arange(S) // CHUNK
    outs = []
    for qb in range(S // Q_BLOCK):
        s0, s1 = qb * Q_BLOCK, (qb + 1) * Q_BLOCK
        mask = jnp.asarray(cid[s0:s1, None] >= cid[None, :s1])
        scores = jnp.einsum('bqhmd,bkhmd->bhmqk', q[:, s0:s1], k[:, :s1]).astype(jnp.float32)
        p = jax.nn.softmax(jnp.where(mask, scores, -jnp.inf), axis=-1)
        a = p[:, :, 0] - lam * p[:, :, 1]
        outs.append(jnp.einsum('bhqk,bkhe->bqhe', a.astype(v.dtype), v[:, :s1]))
    o = jnp.concatenate(outs, axis=1)
    o = rmsnorm(o, subln, eps=1e-5) * (1.0 - lambda_init)
    return o.reshape(B, S, DA_HEADS * DA_V_DIM) @ w_o


def conformer_conv(h, w_in, b_in, dw, dw_b, ln_g, ln_b, w_out, b_out):
    u = h @ w_in + b_in
    a, g = jnp.split(u, 2, axis=-1)
    u = a * jax.nn.sigmoid(g)
    u = causal_dwconv(u, dw) + dw_b
    uf = u.astype(jnp.float32)
    mu = jnp.mean(uf, axis=-1, keepdims=True)
    var = jnp.mean(jnp.square(uf - mu), axis=-1, keepdims=True)
    u = ((uf - mu) * lax.rsqrt(var + 1e-5) * ln_g.astype(jnp.float32) + ln_b.astype(jnp.float32)).astype(h.dtype)
    return jax.nn.silu(u) @ w_out + b_out


def l2norm(x, eps=1e-6):
    return x * lax.rsqrt(jnp.sum(x * x, axis=-1, keepdims=True) + eps)


def chunk_gated_delta_rule(q, k, v, g, beta):
    B, S, H, DK = q.shape
    DV = v.shape[-1]
    C = CHUNK
    N = S // C

    def to_chunks(t):
        return t.reshape(B, N, C, H, -1).transpose(0, 3, 1, 2, 4)

    q, k, v = to_chunks(q), to_chunks(k), to_chunks(v)
    g = g.reshape(B, N, C, H).transpose(0, 3, 1, 2)
    beta = beta.reshape(B, N, C, H).transpose(0, 3, 1, 2)
    gc = jnp.cumsum(g, axis=-1)
    tril = jnp.tril(jnp.ones((C, C), dtype=bool))
    strict = jnp.tril(jnp.ones((C, C), dtype=bool), k=-1)
    decay = jnp.exp(jnp.where(tril, gc[..., :, None] - gc[..., None, :], -jnp.inf))
    k_beta = k * beta[..., None]
    v_beta = v * beta[..., None]
    L = jnp.where(strict, jnp.einsum('bhnid,bhnjd->bhnij', k_beta, k) * decay, 0.0)
    eye = jnp.eye(C, dtype=jnp.float32)
    T = lax.linalg.triangular_solve(eye + L, jnp.broadcast_to(eye, L.shape), left_side=True, lower=True)
    u = jnp.einsum('bhnij,bhnje->bhnie', T, v_beta)
    w = jnp.einsum('bhnij,bhnjd->bhnid', T, k_beta * jnp.exp(gc)[..., None])
    a_qk = jnp.einsum('bhnid,bhnjd->bhnij', q, k) * decay
    xs = tuple(jnp.moveaxis(t, 2, 0) for t in (q, k, u, w, a_qk, gc))

    def step(state, inp):
        q_n, k_n, u_n, w_n, a_n, gc_n = inp
        v_new = u_n - jnp.einsum('bhcd,bhde->bhce', w_n, state)
        o = (jnp.einsum('bhcd,bhde->bhce', q_n * jnp.exp(gc_n)[..., None], state)
             + jnp.einsum('bhij,bhje->bhie', a_n, v_new))
        g_last = gc_n[..., -1]
        state = (state * jnp.exp(g_last)[..., None, None]
                 + jnp.einsum('bhcd,bhce->bhde', k_n * jnp.exp(g_last[..., None] - gc_n)[..., None], v_new))
        return state, o

    state0 = jnp.zeros((B, H, DK, DV), jnp.float32)
    _, o = lax.scan(step, state0, xs)
    return o.transpose(1, 0, 3, 2, 4).reshape(B, S, H, DV)


def gated_deltanet(h, w_in, conv_w, a_log, dt_bias, norm_g, w_o):
    B, S, _ = h.shape
    proj = h @ w_in
    n_qkv = 2 * GDN_KEY_DIM + GDN_VAL_DIM
    qkv = jax.nn.silu(causal_dwconv(proj[..., :n_qkv], conv_w))
    z = proj[..., n_qkv:n_qkv + GDN_VAL_DIM].reshape(B, S, GDN_V_HEADS, GDN_HEAD_DIM)
    b = proj[..., n_qkv + GDN_VAL_DIM:n_qkv + GDN_VAL_DIM + GDN_V_HEADS].astype(jnp.float32)
    a = proj[..., n_qkv + GDN_VAL_DIM + GDN_V_HEADS:].astype(jnp.float32)
    qkv = qkv.astype(jnp.float32)
    q = qkv[..., :GDN_KEY_DIM].reshape(B, S, GDN_K_HEADS, GDN_HEAD_DIM)
    k = qkv[..., GDN_KEY_DIM:2 * GDN_KEY_DIM].reshape(B, S, GDN_K_HEADS, GDN_HEAD_DIM)
    v = qkv[..., 2 * GDN_KEY_DIM:].reshape(B, S, GDN_V_HEADS, GDN_HEAD_DIM)
    rep = GDN_V_HEADS // GDN_K_HEADS
    q = jnp.repeat(l2norm(q) * (GDN_HEAD_DIM ** -0.5), rep, axis=2)
    k = jnp.repeat(l2norm(k), rep, axis=2)
    beta = jax.nn.sigmoid(b)
    g = -jnp.exp(a_log.astype(jnp.float32)) * jax.nn.softplus(a + dt_bias.astype(jnp.float32))
    o = chunk_gated_delta_rule(q, k, v, g, beta).astype(h.dtype)
    o = rmsnorm(o, norm_g) * jax.nn.silu(z)
    return o.reshape(B, S, GDN_VAL_DIM) @ w_o


def conv_ffn(h, w_in, dw, dw_b, w_out):
    u = causal_dwconv(h @ w_in, dw) + dw_b
    g, val = jnp.split(u, 2, axis=-1)
    return (jax.nn.silu(g) * val) @ w_out


def setup_inputs(seed: int = 0) -> dict:
    key = jax.random.key(seed)
    ks = iter(jax.random.split(key, 32))
    f32 = jnp.float32

    def wgt(shape, fan_in):
        return jax.random.normal(next(ks), shape, f32) * fan_in ** -0.5

    def gain(shape):
        return 1.0 + 0.02 * jax.random.normal(next(ks), shape, f32)

    def bias(shape):
        return 0.02 * jax.random.normal(next(ks), shape, f32)

    x = jax.random.normal(next(ks), (BATCH, SEQ, D_MODEL), f32)
    offset = jax.random.randint(next(ks), (BATCH, 1), 0, 4096, dtype=jnp.int32)
    positions = offset + jnp.arange(SEQ, dtype=jnp.int32)[None, :]
    norm_mix = gain((DEPTH, D_MODEL))
    norm_ffn = gain((DEPTH, D_MODEL))
    norm_final = gain((D_MODEL,))
    attn_w_qkv = wgt((N_ATTN, D_MODEL, 4 * DA_HEADS * DA_HEAD_DIM + DA_HEADS * DA_V_DIM), D_MODEL)
    attn_lambda = 0.1 * jax.random.normal(next(ks), (N_ATTN, 4, DA_HEAD_DIM), f32)
    attn_subln = gain((N_ATTN, DA_V_DIM))
    attn_w_o = wgt((N_ATTN, DA_HEADS * DA_V_DIM, D_MODEL), DA_HEADS * DA_V_DIM)
    conv_w_in = wgt((N_CONVM, D_MODEL, 2 * D_MODEL), D_MODEL)
    conv_b_in = bias((N_CONVM, 2 * D_MODEL))
    conv_dw = wgt((N_CONVM, CONV_KERNEL, D_MODEL), CONV_KERNEL)
    conv_dw_b = bias((N_CONVM, D_MODEL))
    conv_ln_g = gain((N_CONVM, D_MODEL))
    conv_ln_b = bias((N_CONVM, D_MODEL))
    conv_w_out = wgt((N_CONVM, D_MODEL, D_MODEL), D_MODEL)
    conv_b_out = bias((N_CONVM, D_MODEL))
    gdn_w_in = wgt((N_GDN, D_MODEL, GDN_IN_DIM), D_MODEL)
    gdn_conv = wgt((N_GDN, GDN_CONV, 2 * GDN_KEY_DIM + GDN_VAL_DIM), GDN_CONV)
    gdn_a_log = jnp.log(jax.random.uniform(next(ks), (N_GDN, GDN_V_HEADS), f32, 1.0, 16.0))
    dt = jnp.exp(jax.random.uniform(next(ks), (N_GDN, GDN_V_HEADS), f32, math.log(1e-3), math.log(1e-1)))
    gdn_dt_bias = dt + jnp.log(-jnp.expm1(-dt))
    gdn_norm = gain((N_GDN, GDN_HEAD_DIM))
    gdn_w_o = wgt((N_GDN, GDN_VAL_DIM, D_MODEL), GDN_VAL_DIM)
    ffn_w_in = wgt((DEPTH, D_MODEL, 2 * D_FF), D_MODEL)
    ffn_dw = wgt((DEPTH, FFN_CONV, 2 * D_FF), FFN_CONV)
    ffn_dw_b = bias((DEPTH, 2 * D_FF))
    ffn_w_out = wgt((DEPTH, D_FF, D_MODEL), D_FF)
    return {"x": x, "positions": positions, "norm_mix": norm_mix, "norm_ffn": norm_ffn,
            "norm_final": norm_final, "attn_w_qkv": attn_w_qkv, "attn_lambda": attn_lambda,
            "attn_subln": attn_subln, "attn_w_o": attn_w_o, "conv_w_in": conv_w_in,
            "conv_b_in": conv_b_in, "conv_dw": conv_dw, "conv_dw_b": conv_dw_b,
            "conv_ln_g": conv_ln_g, "conv_ln_b": conv_ln_b, "conv_w_out": conv_w_out,
            "conv_b_out": conv_b_out, "gdn_w_in": gdn_w_in, "gdn_conv": gdn_conv,
            "gdn_a_log": gdn_a_log, "gdn_dt_bias": gdn_dt_bias, "gdn_norm": gdn_norm,
            "gdn_w_o": gdn_w_o, "ffn_w_in": ffn_w_in, "ffn_dw": ffn_dw, "ffn_dw_b": ffn_dw_b,
            "ffn_w_out": ffn_w_out}


def reference(x, positions, norm_mix, norm_ffn, norm_final, attn_w_qkv, attn_lambda, attn_subln,
              attn_w_o, conv_w_in, conv_b_in, conv_dw, conv_dw_b, conv_ln_g, conv_ln_b, conv_w_out,
              conv_b_out, gdn_w_in, gdn_conv, gdn_a_log, gdn_dt_bias, gdn_norm, gdn_w_o,
              ffn_w_in, ffn_dw, ffn_dw_b, ffn_w_out):
    h = x
    for i in range(DEPTH):
        j = i // N_MIXERS
        kind = i % N_MIXERS
        hn = rmsnorm(h, norm_mix[i])
        if kind == 0:
            mix = diff_attention(hn, positions, attn_w_qkv[j], attn_lambda[j], attn_subln[j],
                                 attn_w_o[j], lambda_init_fn(i))
        elif kind == 1:
            mix = conformer_conv(hn, conv_w_in[j], conv_b_in[j], conv_dw[j], conv_dw_b[j],
                                 conv_ln_g[j], conv_ln_b[j], conv_w_out[j], conv_b_out[j])
        else:
            mix = gated_deltanet(hn, gdn_w_in[j], gdn_conv[j], gdn_a_log[j], gdn_dt_bias[j],
                                 gdn_norm[j], gdn_w_o[j])
        h = h + mix
        h = h + conv_ffn(rmsnorm(h, norm_ffn[i]), ffn_w_in[i], ffn_dw[i], ffn_dw_b[i], ffn_w_out[i])
    return rmsnorm(h, norm_final)
```

```python
import functools
import math

import jax
import jax.numpy as jnp
from jax import lax
from jax.experimental import pallas as pl
from jax.experimental.pallas import tpu as pltpu

F32 = jnp.float32
BF16 = jnp.bfloat16

N_MIXERS = 3
RMS_EPS = 1e-6
CHUNK = 64
DA_HEAD_DIM = 64
DA_V_DIM = 2 * DA_HEAD_DIM
ROPE_THETA = 500000.0
ROPE_DIM = DA_HEAD_DIM // 4
SUBLN_EPS = 1e-5
LN_EPS = 1e-5
GDN_HEAD_DIM = 128
L2_EPS = 1e-6

LANES = 128
SUBLANES = 8
VMEM_LIMIT_BYTES = 56 * 1024 * 1024

SEQ_TILE = 512
ATTN_TQ = 256
ATTN_TK = 256
GDN_GROUP = 4 * CHUNK
GDN_HEADS_PER_STEP = 4
NEG_BIG = -1e30


def _lambda_init(layer):
    return 0.8 - 0.6 * math.exp(-0.3 * layer)


def _rms(x, eps):
    return x * lax.rsqrt(jnp.mean(x * x, axis=-1, keepdims=True) + eps)


def _sigmoid(x):
    return 1.0 / (1.0 + jnp.exp(-x))


def _bdot(a, b):
    return jnp.dot(a.astype(BF16), b.astype(BF16), preferred_element_type=F32)


def _fdot(a, b):
    return jnp.dot(a, b, preferred_element_type=F32, precision=lax.Precision.HIGHEST)


def _const_spec(shape):
    nd = len(shape)
    return pl.BlockSpec(shape, lambda *_: (0,) * nd, pipeline_mode=pl.Buffered(1))


def _params(*sem):
    return pltpu.CompilerParams(dimension_semantics=sem, vmem_limit_bytes=VMEM_LIMIT_BYTES)


def _ffn_kernel(h_ref, x_ref, wx_ref, bx_ref, gn_ref, win_ref, dw_ref, wout_ref, gfin_ref, o_ref,
                carry_ref, u_ref, *, n_chunks, fc, final_norm):
    T = h_ref.shape[1]
    H = SUBLANES

    @pl.when(pl.program_id(1) == 0)
    def _():
        carry_ref[...] = jnp.zeros_like(carry_ref)

    h1 = h_ref[0] + jnp.dot(x_ref[0], wx_ref[...], preferred_element_type=F32) + bx_ref[...]
    hn = (_rms(h1, RMS_EPS) * gn_ref[...]).astype(BF16)
    acc = h1
    for c in range(n_chunks):
        slot = c % 2
        u = jnp.dot(hn, win_ref[c], preferred_element_type=F32)
        u_ref[slot, 0:H, :] = carry_ref[c]
        u_ref[slot, H:H + T, :] = u
        dw = dw_ref[c]
        y = (u * dw[2:3] + u_ref[slot, H - 1:H - 1 + T, :] * dw[1:2]
             + u_ref[slot, H - 2:H - 2 + T, :] * dw[0:1] + dw[3:4])
        carry_ref[c] = u[T - H:T, :]
        g = y[:, :fc]
        act = (g * _sigmoid(g) * y[:, fc:]).astype(BF16)
        acc = acc + jnp.dot(act, wout_ref[c], preferred_element_type=F32)
    if final_norm:
        acc = _rms(acc, RMS_EPS) * gfin_ref[...]
    o_ref[0] = acc


def _ffn_layer(h, x, wx, bx, gn, w_in, dw, dw_b, w_out, gfin, *, final_norm):
    B, S, D = h.shape
    Kx = x.shape[-1]
    d_ff = w_out.shape[0]
    fc = 2 * LANES
    assert d_ff % fc == 0 and S % SEQ_TILE == 0
    n_chunks = d_ff // fc
    T = SEQ_TILE
    w_in_c = jnp.concatenate([w_in[:, :d_ff].reshape(D, n_chunks, fc),
                              w_in[:, d_ff:].reshape(D, n_chunks, fc)], axis=-1)
    w_in_c = w_in_c.transpose(1, 0, 2).astype(BF16)
    dwb = jnp.concatenate([dw, dw_b[None]], axis=0)
    dwb_c = jnp.concatenate([dwb[:, :d_ff].reshape(4, n_chunks, fc),
                             dwb[:, d_ff:].reshape(4, n_chunks, fc)], axis=-1).transpose(1, 0, 2)
    w_out_c = w_out.reshape(n_chunks, fc, D).astype(BF16)
    kern = functools.partial(_ffn_kernel, n_chunks=n_chunks, fc=fc, final_norm=final_norm)
    return pl.pallas_call(
        kern,
        grid=(B, S // T),
        in_specs=[
            pl.BlockSpec((1, T, D), lambda b, s: (b, s, 0)),
            pl.BlockSpec((1, T, Kx), lambda b, s: (b, s, 0)),
            _const_spec((Kx, D)),
            _const_spec((1, D)),
            _const_spec((1, D)),
            _const_spec((n_chunks, D, 2 * fc)),
            _const_spec((n_chunks, 4, 2 * fc)),
            _const_spec((n_chunks, fc, D)),
            _const_spec((1, D)),
        ],
        out_specs=pl.BlockSpec((1, T, D), lambda b, s: (b, s, 0)),
        out_shape=jax.ShapeDtypeStruct((B, S, D), F32),
        scratch_shapes=[pltpu.VMEM((n_chunks, SUBLANES, 2 * fc), F32),
                        pltpu.VMEM((2, SUBLANES + T, 2 * fc), F32)],
        compiler_params=_params("parallel", "arbitrary"),
        name="ffn",
    )(h, x, wx.astype(BF16), bx.reshape(1, D), gn.reshape(1, D), w_in_c, dwb_c, w_out_c,
      gfin.reshape(1, D))


def _attn_qkv_kernel(h_ref, pos_ref, gn_ref, w_ref, freq_ref, q_ref, k_ref, v_ref):
    D = h_ref.shape[2]
    nq = q_ref.shape[2]
    hn = (_rms(h_ref[0], RMS_EPS) * gn_ref[...]).astype(BF16)
    ang = pos_ref[0].astype(F32) * freq_ref[0:1, :]
    cos = jnp.cos(ang)
    sin = jnp.sin(ang)
    first = freq_ref[1:2, :]
    second = freq_ref[2:3, :]
    c_tab = (first + second) * cos + (1.0 - first - second)
    s_hi = -first * sin
    s_lo = second * sin

    def rope(x):
        return (x * c_tab + pltpu.roll(x, LANES - ROPE_DIM // 2, 1) * s_hi
                + pltpu.roll(x, ROPE_DIM // 2, 1) * s_lo)

    q = jnp.dot(hn, w_ref[:, 0:nq], preferred_element_type=F32)
    k = jnp.dot(hn, w_ref[:, nq:2 * nq], preferred_element_type=F32)
    for j in range(nq // LANES):
        sl = slice(j * LANES, (j + 1) * LANES)
        q_ref[0, :, sl] = (rope(q[:, sl]) * (DA_HEAD_DIM ** -0.5)).astype(BF16)
        k_ref[0, :, sl] = rope(k[:, sl]).astype(BF16)
    v_ref[0] = jnp.dot(hn, w_ref[:, 2 * nq:], preferred_element_type=F32).astype(BF16)


def _attn_core_kernel(q_ref, k_ref, v_ref, lam_ref, subln_ref, o_ref, *, lambda_init):
    TQ = q_ref.shape[1]
    TK = ATTN_TK
    qi = pl.program_id(2)
    q = q_ref[0]
    lane = lax.broadcasted_iota(jnp.int32, q.shape, 1)
    zero = jnp.zeros_like(q)
    qq = jnp.concatenate([jnp.where(lane < DA_HEAD_DIM, q, zero),
                          jnp.where(lane >= DA_HEAD_DIM, q, zero)], axis=0)

    def step(kb, carry, masked):
        m, l, acc = carry
        start = pl.multiple_of(kb * TK, TK)
        kblk = k_ref[0, pl.ds(start, TK), :]
        vblk = v_ref[0, pl.ds(start, TK), :]
        s = lax.dot_general(qq, kblk, (((1,), (1,)), ((), ())), preferred_element_type=F32)
        if masked:
            row = lax.broadcasted_iota(jnp.int32, (TQ, TK), 0) // CHUNK
            col = lax.broadcasted_iota(jnp.int32, (TQ, TK), 1) // CHUNK
            keep = col <= row
            keep = jnp.concatenate([keep, keep], axis=0)
            s = jnp.where(keep, s, NEG_BIG)
        m_new = jnp.maximum(m, jnp.max(s, axis=-1, keepdims=True))
        alpha = jnp.exp(m - m_new)
        p = jnp.exp(s - m_new)
        l = alpha * l + jnp.sum(p, axis=-1, keepdims=True)
        acc = alpha * acc + jnp.dot(p.astype(BF16), vblk, preferred_element_type=F32)
        return m_new, l, acc

    init = (jnp.full((2 * TQ, 1), NEG_BIG, F32), jnp.zeros((2 * TQ, 1), F32),
            jnp.zeros((2 * TQ, DA_V_DIM), F32))
    n_full = qi * (TQ // TK)
    carry = lax.fori_loop(0, n_full, lambda kb, c: step(kb, c, False), init)
    for j in range(TQ // TK):
        assert TQ == TK
        carry = step(n_full + j, carry, True)
    _, l, acc = carry
    o = acc / l
    lp = lam_ref[...]
    lam = (jnp.exp(jnp.sum(lp[0:1] * lp[1:2], axis=-1, keepdims=True))
           - jnp.exp(jnp.sum(lp[2:3] * lp[3:4], axis=-1, keepdims=True)) + lambda_init)
    o = o[:TQ] - lam * o[TQ:]
    o = _rms(o, SUBLN_EPS) * subln_ref[...] * (1.0 - lambda_init)
    o_ref[0] = o.astype(BF16)


def _attention_layer(h, positions, gn, w_qkv, lam_p, subln, layer):
    B, S, D = h.shape
    n_heads = w_qkv.shape[1] // (3 * DA_V_DIM)
    nq = n_heads * 2 * DA_HEAD_DIM
    T = SEQ_TILE
    assert S % T == 0 and S % ATTN_TQ == 0 and w_qkv.shape[1] == 3 * nq
    half = ROPE_DIM // 2
    inv_freq = jnp.power(ROPE_THETA, -jnp.arange(half, dtype=F32) * 2.0 / ROPE_DIM)
    d = jnp.arange(LANES) % DA_HEAD_DIM
    freq_tab = jnp.stack([inv_freq[d % half], (d < half).astype(F32),
                          ((d >= half) & (d < ROPE_DIM)).astype(F32)]
                         + [jnp.zeros((LANES,), F32)] * 5)
    q, k, v = pl.pallas_call(
        _attn_qkv_kernel,
        grid=(B, S // T),
        in_specs=[
            pl.BlockSpec((1, T, D), lambda b, s: (b, s, 0)),
            pl.BlockSpec((1, T, 1), lambda b, s: (b, s, 0)),
            _const_spec((1, D)),
            _const_spec((D, 3 * nq)),
            _const_spec((SUBLANES, LANES)),
        ],
        out_specs=[pl.BlockSpec((1, T, nq), lambda b, s: (b, s, 0))] * 3,
        out_shape=[jax.ShapeDtypeStruct((B, S, nq), BF16)] * 3,
        compiler_params=_params("parallel", "parallel"),
        name="attn_qkv",
    )(h, positions.reshape(B, S, 1), gn.reshape(1, D), w_qkv.astype(BF16), freq_tab)

    TQ = ATTN_TQ
    kern = functools.partial(_attn_core_kernel, lambda_init=_lambda_init(layer))
    return pl.pallas_call(
        kern,
        grid=(B, n_heads, S // TQ),
        in_specs=[
            pl.BlockSpec((1, TQ, DA_V_DIM), lambda b, hd, i: (b, i, hd)),
            pl.BlockSpec((1, S, DA_V_DIM), lambda b, hd, i: (b, 0, hd)),
            pl.BlockSpec((1, S, DA_V_DIM), lambda b, hd, i: (b, 0, hd)),
            _const_spec((4, DA_HEAD_DIM)),
            _const_spec((1, DA_V_DIM)),
        ],
        out_specs=pl.BlockSpec((1, TQ, DA_V_DIM), lambda b, hd, i: (b, i, hd)),
        out_shape=jax.ShapeDtypeStruct((B, S, nq), BF16),
        compiler_params=_params("parallel", "parallel", "arbitrary"),
        name="attn_core",
    )(q, k, v, lam_p, subln.reshape(1, DA_V_DIM))


def _conv_mix_kernel(h_ref, gn_ref, win_ref, bin_ref, dw_ref, lng_ref, lnb_ref, o_ref, u_ref, *, n_taps):
    T = h_ref.shape[1]
    D = h_ref.shape[2]
    HALO = 4 * SUBLANES
    RB = 64
    assert n_taps - 1 <= HALO and T % RB == 0

    @pl.when(pl.program_id(1) == 0)
    def _():
        u_ref[0:HALO, :] = jnp.zeros((HALO, D), F32)

    hn = (_rms(h_ref[0], RMS_EPS) * gn_ref[...]).astype(BF16)
    a = jnp.dot(hn, win_ref[:, 0:D], preferred_element_type=F32) + bin_ref[:, 0:D]
    g = jnp.dot(hn, win_ref[:, D:2 * D], preferred_element_type=F32) + bin_ref[:, D:2 * D]
    u_ref[HALO:HALO + T, :] = a * _sigmoid(g)

    for rb in range(T // RB):
        base = HALO + rb * RB - (n_taps - 1)
        y = jnp.zeros((RB, D), F32) + dw_ref[n_taps:n_taps + 1, :]
        for t in range(n_taps):
            y = y + u_ref[base + t:base + t + RB, :] * dw_ref[t:t + 1, :]
        mu = jnp.mean(y, axis=-1, keepdims=True)
        yc = y - mu
        var = jnp.mean(yc * yc, axis=-1, keepdims=True)
        z = yc * lax.rsqrt(var + LN_EPS) * lng_ref[...] + lnb_ref[...]
        o_ref[0, rb * RB:(rb + 1) * RB, :] = (z * _sigmoid(z)).astype(BF16)
    u_ref[0:HALO, :] = u_ref[T:T + HALO, :]


def _conformer_layer(h, gn, w_in, b_in, dw, dw_b, ln_g, ln_b):
    B, S, D = h.shape
    T = SEQ_TILE
    n_taps = dw.shape[0]
    dwb = jnp.concatenate([dw, dw_b[None]], axis=0)
    kern = functools.partial(_conv_mix_kernel, n_taps=n_taps)
    return pl.pallas_call(
        kern,
        grid=(B, S // T),
        in_specs=[
            pl.BlockSpec((1, T, D), lambda b, s: (b, s, 0)),
            _const_spec((1, D)),
            _const_spec((D, 2 * D)),
            _const_spec((1, 2 * D)),
            _const_spec((n_taps + 1, D)),
            _const_spec((1, D)),
            _const_spec((1, D)),
        ],
        out_specs=pl.BlockSpec((1, T, D), lambda b, s: (b, s, 0)),
        out_shape=jax.ShapeDtypeStruct((B, S, D), BF16),
        scratch_shapes=[pltpu.VMEM((4 * SUBLANES + T, D), F32)],
        compiler_params=_params("parallel", "arbitrary"),
        name="conv_mix",
    )(h, gn.reshape(1, D), w_in.astype(BF16), b_in.reshape(1, 2 * D), dwb, ln_g.reshape(1, D),
      ln_b.reshape(1, D))


def _gdn_proj_kernel(h_ref, gn_ref, wqkv_ref, cw_ref, wz_ref, wba_ref, gate_ref,
                     q_ref, k_ref, v_ref, z_ref, bg_ref, carry_ref, u_ref, *, n_taps, key_dim):
    T = h_ref.shape[1]
    H = SUBLANES
    CW = 4 * LANES
    n_qkv = wqkv_ref.shape[1]
    n_chunks = n_qkv // CW
    assert n_taps - 1 <= H and key_dim % CW == 0

    @pl.when(pl.program_id(1) == 0)
    def _():
        carry_ref[...] = jnp.zeros_like(carry_ref)

    hn = (_rms(h_ref[0], RMS_EPS) * gn_ref[...]).astype(BF16)
    for c in range(n_chunks):
        slot = c % 2
        u = jnp.dot(hn, wqkv_ref[:, c * CW:(c + 1) * CW], preferred_element_type=F32)
        u_ref[slot, 0:H, :] = carry_ref[c]
        u_ref[slot, H:H + T, :] = u
        cw = cw_ref[:, c * CW:(c + 1) * CW]
        y = u * cw[n_taps - 1:n_taps]
        for t in range(n_taps - 1):
            off = H - (n_taps - 1) + t
            y = y + u_ref[slot, off:off + T, :] * cw[t:t + 1]
        carry_ref[c] = u[T - H:T, :]
        y = y * _sigmoid(y)
        col = c * CW
        for j in range(CW // LANES):
            yj = y[:, j * LANES:(j + 1) * LANES]
            cj = col + j * LANES
            if cj < 2 * key_dim:
                yj = yj * lax.rsqrt(jnp.sum(yj * yj, axis=-1, keepdims=True) + L2_EPS)
                if cj < key_dim:
                    q_ref[0, :, cj:cj + LANES] = (yj * (GDN_HEAD_DIM ** -0.5)).astype(BF16)
                else:
                    k_ref[0, :, cj - key_dim:cj - key_dim + LANES] = yj.astype(BF16)
            else:
                v_ref[0, :, cj - 2 * key_dim:cj - 2 * key_dim + LANES] = yj.astype(BF16)
    z_ref[0] = jnp.dot(hn, wz_ref[...], preferred_element_type=F32).astype(BF16)
    ba = jnp.dot(hn, wba_ref[...], preferred_element_type=F32)
    is_beta = gate_ref[0:1, :]
    is_g = gate_ref[1:2, :]
    xs = ba + gate_ref[3:4, :]
    softplus = jnp.maximum(xs, 0.0) + jnp.log(1.0 + jnp.exp(-jnp.abs(xs)))
    bg_ref[0] = is_beta * _sigmoid(ba) - is_g * jnp.exp(gate_ref[2:3, :]) * softplus


def _gdn_core_kernel(q_ref, k_ref, kt_ref, v_ref, z_ref, col_ref, row_ref, norm_ref, o_ref, state_ref,
                     *, heads_per_step):
    S = q_ref.shape[1]
    G = GDN_GROUP
    C = CHUNK
    NC = G // C
    HB = heads_per_step
    rep = 2

    state_ref[...] = jnp.zeros_like(state_ref)
    ri = lax.broadcasted_iota(jnp.int32, (G, G), 0)
    ci = lax.broadcasted_iota(jnp.int32, (G, G), 1)
    same = (ri // C) == (ci // C)
    tril = same & (ci <= ri)
    tril_f = tril.astype(F32)
    strict_f = (same & (ci < ri)).astype(F32)
    triu_f = (same & (ci >= ri)).astype(F32)
    same_f = same.astype(F32)
    eye_f = (ri == ci).astype(F32)
    lane_g = lax.broadcasted_iota(jnp.int32, (1, G), 1) // C

    def group(gi, _):
        r0 = pl.multiple_of(gi * G, G)
        cols = col_ref[0, 0, pl.ds(r0, G), :]
        rows = row_ref[0, 0, :, pl.ds(r0, G)]
        gc_cols = _fdot(tril_f, cols)
        gc_rows = _fdot(rows, triu_f)
        glast_rows = _fdot(rows, same_f)
        glast_cols = _fdot(same_f, cols)
        for kh in range(HB // rep):
            ksl = slice(kh * GDN_HEAD_DIM, (kh + 1) * GDN_HEAD_DIM)
            kg = k_ref[0, pl.ds(r0, G), ksl]
            qg = q_ref[0, pl.ds(r0, G), ksl]
            ktg = kt_ref[0, ksl, pl.ds(r0, G)]
            kk = jnp.dot(kg, ktg, preferred_element_type=F32)
            qk = jnp.dot(qg, ktg, preferred_element_type=F32)
            for r in range(rep):
                hl = kh * rep + r
                vsl = slice(hl * GDN_HEAD_DIM, (hl + 1) * GDN_HEAD_DIM)
                beta_c = cols[:, hl:hl + 1]
                gc_c = gc_cols[:, HB + hl:HB + hl + 1]
                glast_c = glast_cols[:, HB + hl:HB + hl + 1]
                gc_r = gc_rows[HB + hl:HB + hl + 1, :]
                glast_r = glast_rows[HB + hl:HB + hl + 1, :]

                decay = jnp.exp(jnp.where(tril, gc_c - gc_r, NEG_BIG))
                lmat = kk * beta_c * decay * strict_f
                aqk = qk * decay
                x = -lmat
                p = eye_f + x
                y = _bdot(x, x)
                n_sq = int(math.log2(C)) - 1
                for it in range(n_sq):
                    if it < n_sq - 1:
                        both = _bdot(y, jnp.concatenate([p, y], axis=1))
                        p = p + both[:, :G]
                        y = both[:, G:]
                    else:
                        p = p + _bdot(y, p)
                vg = v_ref[0, pl.ds(r0, G), vsl].astype(F32)
                egc = jnp.exp(gc_c)
                rhs = jnp.concatenate([vg * beta_c, kg.astype(F32) * (beta_c * egc)], axis=1)
                uw = _bdot(p, rhs)
                u_all = uw[:, :GDN_HEAD_DIM]
                w_all = uw[:, GDN_HEAD_DIM:]
                qe = qg.astype(F32) * egc
                kdec = jnp.exp(glast_r - gc_r)
                st = state_ref[hl]
                vnew = [u_all[c * C:(c + 1) * C] for c in range(NC)]
                o_inter = []
                for c in range(NC):
                    rs = slice(c * C, (c + 1) * C)
                    ws = _bdot(jnp.concatenate([w_all[rs], qe[rs]], axis=0), st)
                    vnew[c] = u_all[rs] - ws[:C]
                    o_inter.append(ws[C:])
                    ktd = jnp.where(lane_g == c, ktg.astype(F32) * kdec, 0.0)
                    e_last = jnp.exp(glast_c[c * C:c * C + 1, :])
                    st = st * e_last + _bdot(ktd, jnp.concatenate(vnew, axis=0))
                state_ref[hl] = st
                o = jnp.concatenate(o_inter, axis=0) + _bdot(aqk, jnp.concatenate(vnew, axis=0))
                zg = z_ref[0, pl.ds(r0, G), vsl].astype(F32)
                o = _rms(o, RMS_EPS) * norm_ref[...] * (zg * _sigmoid(zg))
                o_ref[0, pl.ds(r0, G), vsl] = o.astype(BF16)
        return 0

    lax.fori_loop(0, S // G, group, 0)


def _gdn_layer(h, gn, w_in, conv_w, a_log, dt_bias, norm_g):
    B, S, D = h.shape
    T = SEQ_TILE
    n_vheads = a_log.shape[0]
    val_dim = n_vheads * GDN_HEAD_DIM
    key_dim = val_dim // 2
    n_qkv = 2 * key_dim + val_dim
    n_taps = conv_w.shape[0]
    assert w_in.shape[1] == n_qkv + val_dim + 2 * n_vheads and 2 * n_vheads <= LANES
    w_qkv = w_in[:, :n_qkv].astype(BF16)
    w_z = w_in[:, n_qkv:n_qkv + val_dim].astype(BF16)
    w_ba = jnp.pad(w_in[:, n_qkv + val_dim:], ((0, 0), (0, LANES - 2 * n_vheads))).astype(BF16)
    lane = jnp.arange(LANES)
    zeros_h = jnp.zeros((LANES - 2 * n_vheads,), F32)
    gate_tab = jnp.stack([
        (lane < n_vheads).astype(F32),
        ((lane >= n_vheads) & (lane < 2 * n_vheads)).astype(F32),
        jnp.concatenate([jnp.zeros((n_vheads,), F32), a_log.astype(F32), zeros_h]),
        jnp.concatenate([jnp.zeros((n_vheads,), F32), dt_bias.astype(F32), zeros_h]),
    ] + [jnp.zeros((LANES,), F32)] * 4)
    CW = 4 * LANES
    kern = functools.partial(_gdn_proj_kernel, n_taps=n_taps, key_dim=key_dim)
    q, k, v, z, bg = pl.pallas_call(
        kern,
        grid=(B, S // T),
        in_specs=[
            pl.BlockSpec((1, T, D), lambda b, s: (b, s, 0)),
            _const_spec((1, D)),
            _const_spec((D, n_qkv)),
            _const_spec((n_taps, n_qkv)),
            _const_spec((D, val_dim)),
            _const_spec((D, LANES)),
            _const_spec((SUBLANES, LANES)),
        ],
        out_specs=[
            pl.BlockSpec((1, T, key_dim), lambda b, s: (b, s, 0)),
            pl.BlockSpec((1, T, key_dim), lambda b, s: (b, s, 0)),
            pl.BlockSpec((1, T, val_dim), lambda b, s: (b, s, 0)),
            pl.BlockSpec((1, T, val_dim), lambda b, s: (b, s, 0)),
            pl.BlockSpec((1, T, LANES), lambda b, s: (b, s, 0)),
        ],
        out_shape=[
            jax.ShapeDtypeStruct((B, S, key_dim), BF16),
            jax.ShapeDtypeStruct((B, S, key_dim), BF16),
            jax.ShapeDtypeStruct((B, S, val_dim), BF16),
            jax.ShapeDtypeStruct((B, S, val_dim), BF16),
            jax.ShapeDtypeStruct((B, S, LANES), F32),
        ],
        scratch_shapes=[pltpu.VMEM((n_qkv // CW, SUBLANES, CW), F32),
                        pltpu.VMEM((2, SUBLANES + T, CW), F32)],
        compiler_params=_params("parallel", "arbitrary"),
        name="gdn_proj",
    )(h, gn.reshape(1, D), w_qkv, conv_w, w_z, w_ba, gate_tab)

    HB = GDN_HEADS_PER_STEP
    KB = HB // 2 * GDN_HEAD_DIM
    VB = HB * GDN_HEAD_DIM
    NG = n_vheads // HB
    assert n_vheads % HB == 0 and S % GDN_GROUP == 0
    kt = jnp.swapaxes(k, 1, 2)
    gate_cols = jnp.concatenate([bg[:, :, :n_vheads].reshape(B, S, NG, HB),
                                 bg[:, :, n_vheads:2 * n_vheads].reshape(B, S, NG, HB)], axis=-1)
    gate_cols = gate_cols.transpose(0, 2, 1, 3)
    gate_rows = jnp.swapaxes(gate_cols, 2, 3)
    kern = functools.partial(_gdn_core_kernel, heads_per_step=HB)
    return pl.pallas_call(
        kern,
        grid=(B, NG),
        in_specs=[
            pl.BlockSpec((1, S, KB), lambda b, g: (b, 0, g)),
            pl.BlockSpec((1, S, KB), lambda b, g: (b, 0, g)),
            pl.BlockSpec((1, KB, S), lambda b, g: (b, g, 0)),
            pl.BlockSpec((1, S, VB), lambda b, g: (b, 0, g)),
            pl.BlockSpec((1, S, VB), lambda b, g: (b, 0, g)),
            pl.BlockSpec((1, 1, S, 2 * HB), lambda b, g: (b, g, 0, 0)),
            pl.BlockSpec((1, 1, 2 * HB, S), lambda b, g: (b, g, 0, 0)),
            _const_spec((1, GDN_HEAD_DIM)),
        ],
        out_specs=pl.BlockSpec((1, S, VB), lambda b, g: (b, 0, g)),
        out_shape=jax.ShapeDtypeStruct((B, S, val_dim), BF16),
        scratch_shapes=[pltpu.VMEM((HB, GDN_HEAD_DIM, GDN_HEAD_DIM), F32)],
        compiler_params=_params("parallel", "parallel"),
        name="gdn_core",
    )(q, k, kt, v, z, gate_cols, gate_rows, norm_g.reshape(1, GDN_HEAD_DIM))


def kernel(x, positions, norm_mix, norm_ffn, norm_final, attn_w_qkv, attn_lambda, attn_subln, attn_w_o,
           conv_w_in, conv_b_in, conv_dw, conv_dw_b, conv_ln_g, conv_ln_b, conv_w_out, conv_b_out,
           gdn_w_in, gdn_conv, gdn_a_log, gdn_dt_bias, gdn_norm, gdn_w_o, ffn_w_in, ffn_dw, ffn_dw_b,
           ffn_w_out):
    depth = norm_mix.shape[0]
    d_model = x.shape[-1]
    no_bias = jnp.zeros((d_model,), F32)
    h = x
    for i in range(depth):
        j = i // N_MIXERS
        kind = i % N_MIXERS
        if kind == 0:
            mix = _attention_layer(h, positions, norm_mix[i], attn_w_qkv[j], attn_lambda[j],
                                   attn_subln[j], i)
            w_o, b_o = attn_w_o[j], no_bias
        elif kind == 1:
            mix = _conformer_layer(h, norm_mix[i], conv_w_in[j], conv_b_in[j], conv_dw[j], conv_dw_b[j],
                                   conv_ln_g[j], conv_ln_b[j])
            w_o, b_o = conv_w_out[j], conv_b_out[j]
        else:
            mix = _gdn_layer(h, norm_mix[i], gdn_w_in[j], gdn_conv[j], gdn_a_log[j], gdn_dt_bias[j],
                             gdn_norm[j])
            w_o, b_o = gdn_w_o[j], no_bias
        h = _ffn_layer(h, mix, w_o, b_o, norm_ffn[i], ffn_w_in[i], ffn_dw[i], ffn_dw_b[i], ffn_w_out[i],
                       norm_final, final_norm=(i == depth - 1))
    return h
```

```python
import functools
import math

import jax
import jax.numpy as jnp
from jax import lax
from jax.experimental import pallas as pl
from jax.experimental.pallas import tpu as pltpu

F32 = jnp.float32
BF16 = jnp.bfloat16

N_MIXERS = 3
RMS_EPS = 1e-6
CHUNK = 64
DA_HEAD_DIM = 64
DA_V_DIM = 2 * DA_HEAD_DIM
ROPE_THETA = 500000.0
ROPE_DIM = DA_HEAD_DIM // 4
SUBLN_EPS = 1e-5
LN_EPS = 1e-5
GDN_HEAD_DIM = 128
L2_EPS = 1e-6

LANES = 128
SUBLANES = 8
VMEM_LIMIT_BYTES = 56 * 1024 * 1024

SEQ_TILE = 512
ATTN_TQ = 256
ATTN_TK = 256
GDN_GROUP = 4 * CHUNK
GDN_HEADS_PER_STEP = 4
NEG_BIG = -1e30
LOG2_E = math.log2(math.e)


def _lambda_init(layer):
    return 0.8 - 0.6 * math.exp(-0.3 * layer)


def _rms(x, eps):
    return x * lax.rsqrt(jnp.mean(x * x, axis=-1, keepdims=True) + eps)


def _sigmoid(x):
    return 1.0 / (1.0 + jnp.exp(-x))


def _bdot(a, b):
    return jnp.dot(a.astype(BF16), b.astype(BF16), preferred_element_type=F32)


def _fdot(a, b):
    return jnp.dot(a, b, preferred_element_type=F32, precision=lax.Precision.HIGHEST)


def _const_spec(shape):
    nd = len(shape)
    return pl.BlockSpec(shape, lambda *_: (0,) * nd, pipeline_mode=pl.Buffered(1))


def _params(*sem):
    return pltpu.CompilerParams(dimension_semantics=sem, vmem_limit_bytes=VMEM_LIMIT_BYTES)


def _ffn_kernel(h_ref, x_ref, wx_ref, bx_ref, gn_ref, win_ref, dw_ref, wout_ref, gfin_ref, o_ref,
                carry_ref, u_ref, *, n_chunks, fc, final_norm):
    T = h_ref.shape[1]
    H = SUBLANES

    @pl.when(pl.program_id(1) == 0)
    def _():
        carry_ref[...] = jnp.zeros_like(carry_ref)

    h1 = h_ref[0] + jnp.dot(x_ref[0], wx_ref[...], preferred_element_type=F32) + bx_ref[...]
    hn = (_rms(h1, RMS_EPS) * gn_ref[...]).astype(BF16)
    acc = h1
    for c in range(n_chunks):
        slot = c % 2
        u = jnp.dot(hn, win_ref[c], preferred_element_type=F32)
        u_ref[slot, 0:H, :] = carry_ref[c]
        u_ref[slot, H:H + T, :] = u
        dw = dw_ref[c]
        y = (u * dw[2:3] + u_ref[slot, H - 1:H - 1 + T, :] * dw[1:2]
             + u_ref[slot, H - 2:H - 2 + T, :] * dw[0:1] + dw[3:4])
        carry_ref[c] = u[T - H:T, :]
        g = y[:, :fc]
        act = (g * _sigmoid(g) * y[:, fc:]).astype(BF16)
        acc = acc + jnp.dot(act, wout_ref[c], preferred_element_type=F32)
    if final_norm:
        acc = _rms(acc, RMS_EPS) * gfin_ref[...]
    o_ref[0] = acc


def _ffn_layer(h, x, wx, bx, gn, w_in, dw, dw_b, w_out, gfin, *, final_norm):
    B, S, D = h.shape
    Kx = x.shape[-1]
    d_ff = w_out.shape[0]
    fc = 2 * LANES
    assert d_ff % fc == 0 and S % SEQ_TILE == 0
    n_chunks = d_ff // fc
    T = SEQ_TILE
    w_in_c = jnp.concatenate([w_in[:, :d_ff].reshape(D, n_chunks, fc),
                              w_in[:, d_ff:].reshape(D, n_chunks, fc)], axis=-1)
    w_in_c = w_in_c.transpose(1, 0, 2).astype(BF16)
    dwb = jnp.concatenate([dw, dw_b[None]], axis=0)
    dwb_c = jnp.concatenate([dwb[:, :d_ff].reshape(4, n_chunks, fc),
                             dwb[:, d_ff:].reshape(4, n_chunks, fc)], axis=-1).transpose(1, 0, 2)
    w_out_c = w_out.reshape(n_chunks, fc, D).astype(BF16)
    kern = functools.partial(_ffn_kernel, n_chunks=n_chunks, fc=fc, final_norm=final_norm)
    return pl.pallas_call(
        kern,
        grid=(B, S // T),
        in_specs=[
            pl.BlockSpec((1, T, D), lambda b, s: (b, s, 0)),
            pl.BlockSpec((1, T, Kx), lambda b, s: (b, s, 0)),
            _const_spec((Kx, D)),
            _const_spec((1, D)),
            _const_spec((1, D)),
            _const_spec((n_chunks, D, 2 * fc)),
            _const_spec((n_chunks, 4, 2 * fc)),
            _const_spec((n_chunks, fc, D)),
            _const_spec((1, D)),
        ],
        out_specs=pl.BlockSpec((1, T, D), lambda b, s: (b, s, 0)),
        out_shape=jax.ShapeDtypeStruct((B, S, D), F32),
        scratch_shapes=[pltpu.VMEM((n_chunks, SUBLANES, 2 * fc), F32),
                        pltpu.VMEM((2, SUBLANES + T, 2 * fc), F32)],
        compiler_params=_params("parallel", "arbitrary"),
        name="ffn",
    )(h, x, wx.astype(BF16), bx.reshape(1, D), gn.reshape(1, D), w_in_c, dwb_c, w_out_c,
      gfin.reshape(1, D))


def _attn_qkv_kernel(h_ref, pos_ref, gn_ref, w_ref, wvt_ref, freq_ref, q_ref, k_ref, vt_ref):
    nq = q_ref.shape[2]
    hn = (_rms(h_ref[0], RMS_EPS) * gn_ref[...]).astype(BF16)
    ang = pos_ref[0].astype(F32) * freq_ref[0:1, :]
    cos = jnp.cos(ang)
    sin = jnp.sin(ang)
    first = freq_ref[1:2, :]
    second = freq_ref[2:3, :]
    c_tab = (first + second) * cos + (1.0 - first - second)
    s_hi = -first * sin
    s_lo = second * sin

    def rope(x):
        return (x * c_tab + pltpu.roll(x, LANES - ROPE_DIM // 2, 1) * s_hi
                + pltpu.roll(x, ROPE_DIM // 2, 1) * s_lo)

    q = jnp.dot(hn, w_ref[:, 0:nq], preferred_element_type=F32)
    k = jnp.dot(hn, w_ref[:, nq:2 * nq], preferred_element_type=F32)
    for j in range(nq // LANES):
        sl = slice(j * LANES, (j + 1) * LANES)
        q_ref[0, :, sl] = (rope(q[:, sl]) * (DA_HEAD_DIM ** -0.5 * LOG2_E)).astype(BF16)
        k_ref[0, :, sl] = rope(k[:, sl]).astype(BF16)
    vt_ref[0] = lax.dot_general(wvt_ref[...], hn, (((1,), (1,)), ((), ())),
                                preferred_element_type=F32).astype(BF16)


def _attn_core_kernel(q_ref, k_ref, vt_ref, lam_ref, subln_ref, o_ref, s_ref, acc_ref, *, lambda_init):
    S = q_ref.shape[1]
    TQ = ATTN_TQ
    TK = ATTN_TK
    assert TQ == TK and S % (2 * TQ) == 0
    lp = lam_ref[...]
    lam = (jnp.exp(jnp.sum(lp[0:1] * lp[1:2], axis=-1, keepdims=True))
           - jnp.exp(jnp.sum(lp[2:3] * lp[3:4], axis=-1, keepdims=True)) + lambda_init)
    lane = lax.broadcasted_iota(jnp.int32, (TQ, DA_V_DIM), 1)
    krow = lax.broadcasted_iota(jnp.int32, (TK, 2 * TQ), 0) // CHUNK
    qcol = (lax.broadcasted_iota(jnp.int32, (TK, 2 * TQ), 1) % TQ) // CHUNK
    diag_keep = krow <= qcol

    def scores(qq, kstart):
        kblk = k_ref[0, pl.ds(kstart, TK), :]
        return lax.dot_general(kblk, qq, (((1,), (1,)), ((), ())), preferred_element_type=F32)

    ones_rows = jnp.ones((2 * SUBLANES, TK), BF16)

    def block(st, s, s_max, kstart, m):
        vtblk = jnp.concatenate([vt_ref[0, :, pl.ds(kstart, TK)], ones_rows], axis=0)
        m_new = jnp.maximum(m, s_max)
        alpha = jnp.exp2(m - m_new)
        p = jnp.exp2(s - m_new)
        acc_ref[st] = alpha * acc_ref[st] + jnp.dot(vtblk, p.astype(BF16), preferred_element_type=F32)
        return m_new

    def masked_block(st, s, kstart, m):
        s = jnp.where(diag_keep, s, NEG_BIG)
        return block(st, s, jnp.max(s, axis=0, keepdims=True), kstart, m)

    def finish(st):
        acc = acc_ref[st]
        o = acc[:DA_V_DIM] / acc[DA_V_DIM:DA_V_DIM + 1]
        o = o[:, :TQ] - lam * o[:, TQ:]
        o = o * lax.rsqrt(jnp.mean(o * o, axis=0, keepdims=True) + SUBLN_EPS)
        return (o * subln_ref[...] * (1.0 - lambda_init)).T.astype(BF16)

    def q_tile(t, _):
        row0 = pl.multiple_of(t * 2 * TQ, 2 * TQ)
        qqs = []
        for st in range(2):
            q = q_ref[0, pl.ds(row0 + st * TQ, TQ), :]
            zero = jnp.zeros_like(q)
            qqs.append(jnp.concatenate([jnp.where(lane < DA_HEAD_DIM, q, zero),
                                        jnp.where(lane >= DA_HEAD_DIM, q, zero)], axis=0))
        acc_ref[...] = jnp.zeros_like(acc_ref)
        m0 = jnp.full((1, 2 * TQ), NEG_BIG, F32)

        smax0 = []
        for st in range(2):
            s0 = scores(qqs[st], 0)
            s_ref[0, st] = s0
            smax0.append(jnp.max(s0, axis=0, keepdims=True))

        def one_block(kstart, cur, carry):
            ms, smax = carry
            s_cur = [s_ref[cur, st] for st in range(2)]
            smax_next = []
            for st in range(2):
                s_next = scores(qqs[st], kstart + TK)
                s_ref[1 - cur, st] = s_next
                smax_next.append(jnp.max(s_next, axis=0, keepdims=True))
            ms = tuple(block(st, s_cur[st], smax[st], kstart, ms[st]) for st in range(2))
            return ms, tuple(smax_next)

        def full_blocks(kp, carry):
            kstart = pl.multiple_of(kp * 2 * TK, 2 * TK)
            return one_block(kstart + TK, 1, one_block(kstart, 0, carry))

        (ma, mb), (_, smax_b) = lax.fori_loop(0, t, full_blocks, ((m0, m0), tuple(smax0)))
        s_last = scores(qqs[1], row0 + TK)
        masked_block(0, s_ref[0, 0], row0, ma)
        mb = block(1, s_ref[0, 1], smax_b, row0, mb)
        masked_block(1, s_last, row0 + TK, mb)
        o_ref[0, pl.ds(row0, TQ), :] = finish(0)
        o_ref[0, pl.ds(row0 + TQ, TQ), :] = finish(1)
        return 0

    lax.fori_loop(0, S // (2 * TQ), q_tile, 0)


def _attention_layer(h, positions, gn, w_qkv, lam_p, subln, layer):
    B, S, D = h.shape
    n_heads = w_qkv.shape[1] // (3 * DA_V_DIM)
    nq = n_heads * 2 * DA_HEAD_DIM
    T = SEQ_TILE
    assert S % T == 0 and S % ATTN_TQ == 0 and w_qkv.shape[1] == 3 * nq
    half = ROPE_DIM // 2
    inv_freq = jnp.power(ROPE_THETA, -jnp.arange(half, dtype=F32) * 2.0 / ROPE_DIM)
    d = jnp.arange(LANES) % DA_HEAD_DIM
    freq_tab = jnp.stack([inv_freq[d % half], (d < half).astype(F32),
                          ((d >= half) & (d < ROPE_DIM)).astype(F32)]
                         + [jnp.zeros((LANES,), F32)] * 5)
    w_qk = w_qkv[:, :2 * nq].astype(BF16)
    w_vt = w_qkv[:, 2 * nq:].T.astype(BF16)
    q, k, vt = pl.pallas_call(
        _attn_qkv_kernel,
        grid=(B, S // T),
        in_specs=[
            pl.BlockSpec((1, T, D), lambda b, s: (b, s, 0)),
            pl.BlockSpec((1, T, 1), lambda b, s: (b, s, 0)),
            _const_spec((1, D)),
            _const_spec((D, 2 * nq)),
            _const_spec((nq, D)),
            _const_spec((SUBLANES, LANES)),
        ],
        out_specs=[pl.BlockSpec((1, T, nq), lambda b, s: (b, s, 0)),
                   pl.BlockSpec((1, T, nq), lambda b, s: (b, s, 0)),
                   pl.BlockSpec((1, nq, T), lambda b, s: (b, 0, s))],
        out_shape=[jax.ShapeDtypeStruct((B, S, nq), BF16), jax.ShapeDtypeStruct((B, S, nq), BF16),
                   jax.ShapeDtypeStruct((B, nq, S), BF16)],
        compiler_params=_params("parallel", "parallel"),
        name="attn_qkv",
    )(h, positions.reshape(B, S, 1), gn.reshape(1, D), w_qk, w_vt, freq_tab)

    kern = functools.partial(_attn_core_kernel, lambda_init=_lambda_init(layer))
    return pl.pallas_call(
        kern,
        grid=(B, n_heads),
        in_specs=[
            pl.BlockSpec((1, S, DA_V_DIM), lambda b, hd: (b, 0, hd)),
            pl.BlockSpec((1, S, DA_V_DIM), lambda b, hd: (b, 0, hd)),
            pl.BlockSpec((1, DA_V_DIM, S), lambda b, hd: (b, hd, 0)),
            _const_spec((4, DA_HEAD_DIM)),
            _const_spec((DA_V_DIM, 1)),
        ],
        out_specs=pl.BlockSpec((1, S, DA_V_DIM), lambda b, hd: (b, 0, hd)),
        out_shape=jax.ShapeDtypeStruct((B, S, nq), BF16),
        scratch_shapes=[pltpu.VMEM((2, 2, ATTN_TK, 2 * ATTN_TQ), F32),
                        pltpu.VMEM((2, DA_V_DIM + 2 * SUBLANES, 2 * ATTN_TQ), F32)],
        compiler_params=_params("parallel", "parallel"),
        name="attn_core",
    )(q, k, vt, lam_p, subln.reshape(DA_V_DIM, 1))


def _conv_mix_kernel(h_ref, gn_ref, win_ref, bin_ref, dw_ref, lng_ref, lnb_ref, o_ref, u_ref, *, n_taps):
    T = h_ref.shape[1]
    D = h_ref.shape[2]
    HALO = 4 * SUBLANES
    RB = 64
    assert n_taps - 1 <= HALO and T % RB == 0

    @pl.when(pl.program_id(1) == 0)
    def _():
        u_ref[0:HALO, :] = jnp.zeros((HALO, D), F32)

    hn = (_rms(h_ref[0], RMS_EPS) * gn_ref[...]).astype(BF16)
    a = jnp.dot(hn, win_ref[:, 0:D], preferred_element_type=F32) + bin_ref[:, 0:D]
    g = jnp.dot(hn, win_ref[:, D:2 * D], preferred_element_type=F32) + bin_ref[:, D:2 * D]
    u_ref[HALO:HALO + T, :] = a * _sigmoid(g)

    for rb in range(T // RB):
        base = HALO + rb * RB - (n_taps - 1)
        y = jnp.zeros((RB, D), F32) + dw_ref[n_taps:n_taps + 1, :]
        for t in range(n_taps):
            y = y + u_ref[base + t:base + t + RB, :] * dw_ref[t:t + 1, :]
        mu = jnp.mean(y, axis=-1, keepdims=True)
        yc = y - mu
        var = jnp.mean(yc * yc, axis=-1, keepdims=True)
        z = yc * lax.rsqrt(var + LN_EPS) * lng_ref[...] + lnb_ref[...]
        o_ref[0, rb * RB:(rb + 1) * RB, :] = (z * _sigmoid(z)).astype(BF16)
    u_ref[0:HALO, :] = u_ref[T:T + HALO, :]


def _conformer_layer(h, gn, w_in, b_in, dw, dw_b, ln_g, ln_b):
    B, S, D = h.shape
    T = SEQ_TILE
    n_taps = dw.shape[0]
    dwb = jnp.concatenate([dw, dw_b[None]], axis=0)
    kern = functools.partial(_conv_mix_kernel, n_taps=n_taps)
    return pl.pallas_call(
        kern,
        grid=(B, S // T),
        in_specs=[
            pl.BlockSpec((1, T, D), lambda b, s: (b, s, 0)),
            _const_spec((1, D)),
            _const_spec((D, 2 * D)),
            _const_spec((1, 2 * D)),
            _const_spec((n_taps + 1, D)),
            _const_spec((1, D)),
            _const_spec((1, D)),
        ],
        out_specs=pl.BlockSpec((1, T, D), lambda b, s: (b, s, 0)),
        out_shape=jax.ShapeDtypeStruct((B, S, D), BF16),
        scratch_shapes=[pltpu.VMEM((4 * SUBLANES + T, D), F32)],
        compiler_params=_params("parallel", "arbitrary"),
        name="conv_mix",
    )(h, gn.reshape(1, D), w_in.astype(BF16), b_in.reshape(1, 2 * D), dwb, ln_g.reshape(1, D),
      ln_b.reshape(1, D))


def _gdn_proj_kernel(h_ref, gn_ref, wqkv_ref, cw_ref, wz_ref, wba_ref, gate_ref,
                     q_ref, k_ref, v_ref, z_ref, bg_ref, carry_ref, u_ref, *, n_taps, key_dim, n_vheads):
    T = h_ref.shape[1]
    H = SUBLANES
    CW = 4 * LANES
    n_qkv = wqkv_ref.shape[1]
    n_chunks = n_qkv // CW
    assert n_taps - 1 <= H and key_dim % CW == 0

    @pl.when(pl.program_id(1) == 0)
    def _():
        carry_ref[...] = jnp.zeros_like(carry_ref)

    hn = (_rms(h_ref[0], RMS_EPS) * gn_ref[...]).astype(BF16)
    for c in range(n_chunks):
        slot = c % 2
        u = jnp.dot(hn, wqkv_ref[:, c * CW:(c + 1) * CW], preferred_element_type=F32)
        u_ref[slot, 0:H, :] = carry_ref[c]
        u_ref[slot, H:H + T, :] = u
        cw = cw_ref[:, c * CW:(c + 1) * CW]
        y = u * cw[n_taps - 1:n_taps]
        for t in range(n_taps - 1):
            off = H - (n_taps - 1) + t
            y = y + u_ref[slot, off:off + T, :] * cw[t:t + 1]
        carry_ref[c] = u[T - H:T, :]
        y = y * _sigmoid(y)
        col = c * CW
        for j in range(CW // LANES):
            yj = y[:, j * LANES:(j + 1) * LANES]
            cj = col + j * LANES
            if cj < 2 * key_dim:
                yj = yj * lax.rsqrt(jnp.sum(yj * yj, axis=-1, keepdims=True) + L2_EPS)
                if cj < key_dim:
                    q_ref[0, :, cj:cj + LANES] = (yj * (GDN_HEAD_DIM ** -0.5)).astype(BF16)
                else:
                    k_ref[0, :, cj - key_dim:cj - key_dim + LANES] = yj.astype(BF16)
            else:
                v_ref[0, :, cj - 2 * key_dim:cj - 2 * key_dim + LANES] = yj.astype(BF16)
    z_ref[0] = jnp.dot(hn, wz_ref[...], preferred_element_type=F32).astype(BF16)
    ba = jnp.dot(hn, wba_ref[...], preferred_element_type=F32)
    is_beta = gate_ref[0:1, :]
    is_g = gate_ref[1:2, :]
    xs = ba + gate_ref[3:4, :]
    softplus = jnp.maximum(xs, 0.0) + jnp.log(1.0 + jnp.exp(-jnp.abs(xs)))
    g = -is_g * jnp.exp(gate_ref[2:3, :]) * softplus
    ri = lax.broadcasted_iota(jnp.int32, (T, T), 0)
    ci = lax.broadcasted_iota(jnp.int32, (T, T), 1)
    same = (ri // CHUNK) == (ci // CHUNK)
    gc = _fdot((same & (ci <= ri)).astype(F32), g)
    glast = _fdot(same.astype(F32), g)
    bg_ref[0] = is_beta * _sigmoid(ba) + gc + pltpu.roll(glast, n_vheads, 1)


def _split(a):
    hi = a.astype(BF16)
    return hi, (a - hi.astype(F32)).astype(BF16)


def _mm3(ah, al, bh, bl):
    d = functools.partial(jnp.dot, preferred_element_type=F32)
    return d(ah, bh) + d(al, bh) + d(ah, bl)


def _gdn_core_kernel(q_ref, k_ref, kt_ref, v_ref, z_ref, col_ref, row_ref, norm_ref, o_ref, state_ref,
                     *, heads_per_step):
    S = q_ref.shape[1]
    G = GDN_GROUP
    C = CHUNK
    P = 2 * C
    NP = G // P
    HB = heads_per_step
    rep = 2
    DH = GDN_HEAD_DIM
    n_rounds = int(math.log2(C)) - 1

    state_ref[...] = jnp.zeros_like(state_ref)
    ri = lax.broadcasted_iota(jnp.int32, (P, P), 0)
    ci = lax.broadcasted_iota(jnp.int32, (P, P), 1)
    same = (ri // C) == (ci // C)
    tril = same & (ci <= ri)
    strict_f = (same & (ci < ri)).astype(F32)
    eye_f = (ri == ci).astype(F32)
    lane_c = lax.broadcasted_iota(jnp.int32, (1, P), 1) // C

    def group(gi, _):
        r0 = pl.multiple_of(gi * G, G)
        probs = [(hl, pi) for hl in range(HB) for pi in range(NP)]
        x_l, aqk_l, rhs_l, qe_l, ktd_l, elast_l = {}, {}, {}, {}, {}, {}
        for kh in range(HB // rep):
            ksl = slice(kh * DH, (kh + 1) * DH)
            for pi in range(NP):
                rp = r0 + pi * P
                kg = k_ref[0, pl.ds(rp, P), ksl]
                qg = q_ref[0, pl.ds(rp, P), ksl]
                ktg = kt_ref[0, ksl, pl.ds(rp, P)]
                kk = jnp.dot(kg, ktg, preferred_element_type=F32)
                qk = jnp.dot(qg, ktg, preferred_element_type=F32)
                cols = col_ref[0, 0, pl.ds(rp, P), :]
                rows = row_ref[0, 0, :, pl.ds(rp, P)]
                for r in range(rep):
                    hl = kh * rep + r
                    beta_c = cols[:, hl:hl + 1]
                    gc_c = cols[:, HB + hl:HB + hl + 1]
                    glast_c = cols[:, 2 * HB + hl:2 * HB + hl + 1]
                    gc_r = rows[HB + hl:HB + hl + 1, :]
                    glast_r = rows[2 * HB + hl:2 * HB + hl + 1, :]
                    decay = jnp.exp(jnp.where(tril, gc_c - gc_r, NEG_BIG))
                    x_l[hl, pi] = -(kk * beta_c * decay * strict_f)
                    aqk_l[hl, pi] = (qk * decay).astype(BF16)
                    egc = jnp.exp(gc_c)
                    vg = v_ref[0, pl.ds(rp, P), hl * DH:(hl + 1) * DH].astype(F32)
                    rhs_l[hl, pi] = jnp.concatenate([vg * beta_c, kg.astype(F32) * (beta_c * egc)],
                                                    axis=1).astype(BF16)
                    qe_l[hl, pi] = qg.astype(F32) * egc
                    ktd = ktg.astype(F32) * jnp.exp(glast_r - gc_r)
                    ktd_l[hl, pi] = [jnp.where(lane_c == c, ktd, 0.0).astype(BF16) for c in range(2)]
                    elast_l[hl, pi] = [jnp.exp(glast_c[c * C:c * C + 1, :]) for c in range(2)]
        p_l, y_l = {}, {}
        for pr in probs:
            xh, xl = _split(x_l[pr])
            p_l[pr] = eye_f + x_l[pr]
            y_l[pr] = _mm3(xh, xl, xh, xl)
        for it in range(n_rounds):
            for pr in probs:
                yh, yl = _split(y_l[pr])
                ph, pl_ = _split(p_l[pr])
                if it < n_rounds - 1:
                    both = _mm3(yh, yl, jnp.concatenate([ph, yh], axis=1), jnp.concatenate([pl_, yl], axis=1))
                    p_l[pr] = p_l[pr] + both[:, :P]
                    y_l[pr] = both[:, P:]
                else:
                    p_l[pr] = p_l[pr] + _mm3(yh, yl, ph, pl_)
        an_l, ob_l = {}, {}
        for pr in probs:
            uw = jnp.dot(p_l[pr].astype(BF16), rhs_l[pr], preferred_element_type=F32).astype(BF16)
            ob_l[pr] = jnp.dot(aqk_l[pr], uw, preferred_element_type=F32)
            an_l[pr] = [jnp.dot(ktd_l[pr][c], uw, preferred_element_type=F32) for c in range(2)]
        sts = [state_ref[hl] for hl in range(HB)]
        o_l = {hl: [] for hl in range(HB)}
        for pi in range(NP):
            for c in range(2):
                for hl in range(HB):
                    pr = (hl, pi)
                    rs = slice(c * C, (c + 1) * C)
                    an = an_l[pr][c]
                    qp = qe_l[pr][rs] - ob_l[pr][rs, DH:]
                    lhs = jnp.concatenate([an[:, DH:], qp], axis=0).astype(BF16)
                    res = jnp.dot(lhs, sts[hl].astype(BF16), preferred_element_type=F32)
                    o_l[hl].append(ob_l[pr][rs, :DH] + res[DH:])
                    sts[hl] = sts[hl] * elast_l[pr][c] + an[:, :DH] - res[:DH]
        for hl in range(HB):
            state_ref[hl] = sts[hl]
            vsl = slice(hl * DH, (hl + 1) * DH)
            o = jnp.concatenate(o_l[hl], axis=0)
            zg = z_ref[0, pl.ds(r0, G), vsl].astype(F32)
            o = _rms(o, RMS_EPS) * norm_ref[...] * (zg * _sigmoid(zg))
            o_ref[0, pl.ds(r0, G), vsl] = o.astype(BF16)
        return 0

    lax.fori_loop(0, S // G, group, 0)


def _gdn_layer(h, gn, w_in, conv_w, a_log, dt_bias, norm_g):
    B, S, D = h.shape
    T = SEQ_TILE
    n_vheads = a_log.shape[0]
    val_dim = n_vheads * GDN_HEAD_DIM
    key_dim = val_dim // 2
    n_qkv = 2 * key_dim + val_dim
    n_taps = conv_w.shape[0]
    assert w_in.shape[1] == n_qkv + val_dim + 2 * n_vheads and 3 * n_vheads <= LANES
    w_qkv = w_in[:, :n_qkv].astype(BF16)
    w_z = w_in[:, n_qkv:n_qkv + val_dim].astype(BF16)
    w_ba = jnp.pad(w_in[:, n_qkv + val_dim:], ((0, 0), (0, LANES - 2 * n_vheads))).astype(BF16)
    lane = jnp.arange(LANES)
    zeros_h = jnp.zeros((LANES - 2 * n_vheads,), F32)
    gate_tab = jnp.stack([
        (lane < n_vheads).astype(F32),
        ((lane >= n_vheads) & (lane < 2 * n_vheads)).astype(F32),
        jnp.concatenate([jnp.zeros((n_vheads,), F32), a_log.astype(F32), zeros_h]),
        jnp.concatenate([jnp.zeros((n_vheads,), F32), dt_bias.astype(F32), zeros_h]),
    ] + [jnp.zeros((LANES,), F32)] * 4)
    CW = 4 * LANES
    kern = functools.partial(_gdn_proj_kernel, n_taps=n_taps, key_dim=key_dim, n_vheads=n_vheads)
    q, k, v, z, bg = pl.pallas_call(
        kern,
        grid=(B, S // T),
        in_specs=[
            pl.BlockSpec((1, T, D), lambda b, s: (b, s, 0)),
            _const_spec((1, D)),
            _const_spec((D, n_qkv)),
            _const_spec((n_taps, n_qkv)),
            _const_spec((D, val_dim)),
            _const_spec((D, LANES)),
            _const_spec((SUBLANES, LANES)),
        ],
        out_specs=[
            pl.BlockSpec((1, T, key_dim), lambda b, s: (b, s, 0)),
            pl.BlockSpec((1, T, key_dim), lambda b, s: (b, s, 0)),
            pl.BlockSpec((1, T, val_dim), lambda b, s: (b, s, 0)),
            pl.BlockSpec((1, T, val_dim), lambda b, s: (b, s, 0)),
            pl.BlockSpec((1, T, LANES), lambda b, s: (b, s, 0)),
        ],
        out_shape=[
            jax.ShapeDtypeStruct((B, S, key_dim), BF16),
            jax.ShapeDtypeStruct((B, S, key_dim), BF16),
            jax.ShapeDtypeStruct((B, S, val_dim), BF16),
            jax.ShapeDtypeStruct((B, S, val_dim), BF16),
            jax.ShapeDtypeStruct((B, S, LANES), F32),
        ],
        scratch_shapes=[pltpu.VMEM((n_qkv // CW, SUBLANES, CW), F32),
                        pltpu.VMEM((2, SUBLANES + T, CW), F32)],
        compiler_params=_params("parallel", "arbitrary"),
        name="gdn_proj",
    )(h, gn.reshape(1, D), w_qkv, conv_w, w_z, w_ba, gate_tab)

    HB = GDN_HEADS_PER_STEP
    KB = HB // 2 * GDN_HEAD_DIM
    VB = HB * GDN_HEAD_DIM
    NG = n_vheads // HB
    assert n_vheads % HB == 0 and S % GDN_GROUP == 0
    kt = jnp.swapaxes(k, 1, 2)
    gate_cols = jnp.concatenate([bg[:, :, i * n_vheads:(i + 1) * n_vheads].reshape(B, S, NG, HB)
                                 for i in range(3)], axis=-1)
    gate_cols = gate_cols.transpose(0, 2, 1, 3)
    gate_rows = jnp.swapaxes(gate_cols, 2, 3)
    kern = functools.partial(_gdn_core_kernel, heads_per_step=HB)
    return pl.pallas_call(
        kern,
        grid=(B, NG),
        in_specs=[
            pl.BlockSpec((1, S, KB), lambda b, g: (b, 0, g)),
            pl.BlockSpec((1, S, KB), lambda b, g: (b, 0, g)),
            pl.BlockSpec((1, KB, S), lambda b, g: (b, g, 0)),
            pl.BlockSpec((1, S, VB), lambda b, g: (b, 0, g)),
            pl.BlockSpec((1, S, VB), lambda b, g: (b, 0, g)),
            pl.BlockSpec((1, 1, S, 3 * HB), lambda b, g: (b, g, 0, 0)),
            pl.BlockSpec((1, 1, 3 * HB, S), lambda b, g: (b, g, 0, 0)),
            _const_spec((1, GDN_HEAD_DIM)),
        ],
        out_specs=pl.BlockSpec((1, S, VB), lambda b, g: (b, 0, g)),
        out_shape=jax.ShapeDtypeStruct((B, S, val_dim), BF16),
        scratch_shapes=[pltpu.VMEM((HB, GDN_HEAD_DIM, GDN_HEAD_DIM), F32)],
        compiler_params=_params("parallel", "parallel"),
        name="gdn_core",
    )(q, k, kt, v, z, gate_cols, gate_rows, norm_g.reshape(1, GDN_HEAD_DIM))


def kernel(x, positions, norm_mix, norm_ffn, norm_final, attn_w_qkv, attn_lambda, attn_subln, attn_w_o,
           conv_w_in, conv_b_in, conv_dw, conv_dw_b, conv_ln_g, conv_ln_b, conv_w_out, conv_b_out,
           gdn_w_in, gdn_conv, gdn_a_log, gdn_dt_bias, gdn_norm, gdn_w_o, ffn_w_in, ffn_dw, ffn_dw_b,
           ffn_w_out):
    depth = norm_mix.shape[0]
    d_model = x.shape[-1]
    no_bias = jnp.zeros((d_model,), F32)
    h = x
    for i in range(depth):
        j = i // N_MIXERS
        kind = i % N_MIXERS
        if kind == 0:
            mix = _attention_layer(h, positions, norm_mix[i], attn_w_qkv[j], attn_lambda[j],
                                   attn_subln[j], i)
            w_o, b_o = attn_w_o[j], no_bias
        elif kind == 1:
            mix = _conformer_layer(h, norm_mix[i], conv_w_in[j], conv_b_in[j], conv_dw[j], conv_dw_b[j],
                                   conv_ln_g[j], conv_ln_b[j])
            w_o, b_o = conv_w_out[j], conv_b_out[j]
        else:
            mix = _gdn_layer(h, norm_mix[i], gdn_w_in[j], gdn_conv[j], gdn_a_log[j], gdn_dt_bias[j],
                             gdn_norm[j])
            w_o, b_o = gdn_w_o[j], no_bias
        h = _ffn_layer(h, mix, w_o, b_o, norm_ffn[i], ffn_w_in[i], ffn_dw[i], ffn_dw_b[i], ffn_w_out[i],
                       norm_final, final_norm=(i == depth - 1))
    return h
```

```python
import functools
import math

import jax
import jax.numpy as jnp
from jax import lax
from jax.experimental import pallas as pl
from jax.experimental.pallas import tpu as pltpu

F32 = jnp.float32
BF16 = jnp.bfloat16

N_MIXERS = 3
RMS_EPS = 1e-6
CHUNK = 64
DA_HEAD_DIM = 64
DA_V_DIM = 2 * DA_HEAD_DIM
ROPE_THETA = 500000.0
ROPE_DIM = DA_HEAD_DIM // 4
SUBLN_EPS = 1e-5
LN_EPS = 1e-5
GDN_HEAD_DIM = 128
L2_EPS = 1e-6

LANES = 128
SUBLANES = 8
VMEM_LIMIT_BYTES = 56 * 1024 * 1024

SEQ_TILE = 512
ATTN_TQ = 256
ATTN_TK = 256
GDN_GROUP = 4 * CHUNK
GDN_HEADS_PER_STEP = 4
NEG_BIG = -1e30
LOG2_E = math.log2(math.e)


def _lambda_init(layer):
    return 0.8 - 0.6 * math.exp(-0.3 * layer)


def _rms(x, eps):
    return x * lax.rsqrt(jnp.mean(x * x, axis=-1, keepdims=True) + eps)


def _sigmoid(x):
    return 1.0 / (1.0 + jnp.exp(-x))


def _bdot(a, b):
    return jnp.dot(a.astype(BF16), b.astype(BF16), preferred_element_type=F32)


def _fdot(a, b):
    return jnp.dot(a, b, preferred_element_type=F32, precision=lax.Precision.HIGHEST)


def _const_spec(shape):
    nd = len(shape)
    return pl.BlockSpec(shape, lambda *_: (0,) * nd, pipeline_mode=pl.Buffered(1))


def _params(*sem):
    return pltpu.CompilerParams(dimension_semantics=sem, vmem_limit_bytes=VMEM_LIMIT_BYTES)


def _ffn_kernel(h_ref, x_ref, wx_ref, bx_ref, gn_ref, win_ref, dw_ref, wout_ref, gfin_ref, o_ref,
                carry_ref, act_ref, *, n_chunks, fc, final_norm):
    T = h_ref.shape[1]
    H = SUBLANES

    @pl.when(pl.program_id(1) == 0)
    def _():
        carry_ref[...] = jnp.zeros_like(carry_ref)

    first_row = lax.broadcasted_iota(jnp.int32, (H, 1), 0) == 0

    def shift_rows(a, prev_tail):
        r = pltpu.roll(a, 1, 0)
        return jnp.concatenate([jnp.where(first_row, prev_tail[H - 1:H, :], r[0:H]), r[H:]], axis=0)

    h1 = h_ref[0] + jnp.dot(x_ref[0], wx_ref[...], preferred_element_type=F32) + bx_ref[...]
    hn = (_rms(h1, RMS_EPS) * gn_ref[...]).astype(BF16)
    u_next = jnp.dot(hn, win_ref[0], preferred_element_type=F32)
    for c in range(n_chunks):
        u = u_next
        if c + 1 < n_chunks:
            u_next = jnp.dot(hn, win_ref[c + 1], preferred_element_type=F32)
        dw = dw_ref[c]
        z = u * dw[1:2] + shift_rows(u, carry_ref[c, 0]) * dw[0:1]
        y = u * dw[2:3] + shift_rows(z, carry_ref[c, 1]) + dw[3:4]
        carry_ref[c, 0] = u[T - H:T, :]
        carry_ref[c, 1] = z[T - H:T, :]
        g = y[:, :fc]
        act_ref[:, c * fc:(c + 1) * fc] = (g * _sigmoid(g) * y[:, fc:]).astype(BF16)
    acc = h1 + jnp.dot(act_ref[...], wout_ref[...], preferred_element_type=F32)
    if final_norm:
        acc = _rms(acc, RMS_EPS) * gfin_ref[...]
    o_ref[0] = acc


def _ffn_layer(h, x, wx, bx, gn, w_in, dw, dw_b, w_out, gfin, *, final_norm):
    B, S, D = h.shape
    Kx = x.shape[-1]
    d_ff = w_out.shape[0]
    fc = 2 * LANES
    assert d_ff % fc == 0 and S % SEQ_TILE == 0
    n_chunks = d_ff // fc
    T = SEQ_TILE
    w_in_c = jnp.concatenate([w_in[:, :d_ff].reshape(D, n_chunks, fc),
                              w_in[:, d_ff:].reshape(D, n_chunks, fc)], axis=-1)
    w_in_c = w_in_c.transpose(1, 0, 2).astype(BF16)
    dwb = jnp.concatenate([dw, dw_b[None]], axis=0)
    dwb_c = jnp.concatenate([dwb[:, :d_ff].reshape(4, n_chunks, fc),
                             dwb[:, d_ff:].reshape(4, n_chunks, fc)], axis=-1).transpose(1, 0, 2)
    w_out_c = w_out.astype(BF16)
    kern = functools.partial(_ffn_kernel, n_chunks=n_chunks, fc=fc, final_norm=final_norm)
    return pl.pallas_call(
        kern,
        grid=(B, S // T),
        in_specs=[
            pl.BlockSpec((1, T, D), lambda b, s: (b, s, 0)),
            pl.BlockSpec((1, T, Kx), lambda b, s: (b, s, 0)),
            _const_spec((Kx, D)),
            _const_spec((1, D)),
            _const_spec((1, D)),
            _const_spec((n_chunks, D, 2 * fc)),
            _const_spec((n_chunks, 4, 2 * fc)),
            _const_spec((d_ff, D)),
            _const_spec((1, D)),
        ],
        out_specs=pl.BlockSpec((1, T, D), lambda b, s: (b, s, 0)),
        out_shape=jax.ShapeDtypeStruct((B, S, D), F32),
        scratch_shapes=[pltpu.VMEM((n_chunks, 2, SUBLANES, 2 * fc), F32),
                        pltpu.VMEM((T, d_ff), BF16)],
        compiler_params=_params("parallel", "arbitrary"),
        name="ffn",
    )(h, x, wx.astype(BF16), bx.reshape(1, D), gn.reshape(1, D), w_in_c, dwb_c, w_out_c,
      gfin.reshape(1, D))


def _attn_qkv_kernel(h_ref, pos_ref, gn_ref, w_ref, wvt_ref, freq_ref, q_ref, k_ref, vt_ref):
    nq = q_ref.shape[2]
    hn = (_rms(h_ref[0], RMS_EPS) * gn_ref[...]).astype(BF16)
    ang = pos_ref[0].astype(F32) * freq_ref[0:1, :]
    cos = jnp.cos(ang)
    sin = jnp.sin(ang)
    first = freq_ref[1:2, :]
    second = freq_ref[2:3, :]
    c_tab = (first + second) * cos + (1.0 - first - second)
    s_hi = -first * sin
    s_lo = second * sin

    def rope(x):
        return (x * c_tab + pltpu.roll(x, LANES - ROPE_DIM // 2, 1) * s_hi
                + pltpu.roll(x, ROPE_DIM // 2, 1) * s_lo)

    q = jnp.dot(hn, w_ref[:, 0:nq], preferred_element_type=F32)
    k = jnp.dot(hn, w_ref[:, nq:2 * nq], preferred_element_type=F32)
    for j in range(nq // LANES):
        sl = slice(j * LANES, (j + 1) * LANES)
        q_ref[0, :, sl] = (rope(q[:, sl]) * (DA_HEAD_DIM ** -0.5 * LOG2_E)).astype(BF16)
        k_ref[0, :, sl] = rope(k[:, sl]).astype(BF16)
    vt_ref[0] = lax.dot_general(wvt_ref[...], hn, (((1,), (1,)), ((), ())),
                                preferred_element_type=F32).astype(BF16)


def _attn_core_kernel(q_ref, k_ref, vt_ref, lam_ref, subln_ref, o_ref, s_ref, acc_ref, *, lambda_init):
    S = q_ref.shape[1]
    TQ = ATTN_TQ
    TK = ATTN_TK
    assert TQ == TK and S % (2 * TQ) == 0
    lp = lam_ref[...]
    lam = (jnp.exp(jnp.sum(lp[0:1] * lp[1:2], axis=-1, keepdims=True))
           - jnp.exp(jnp.sum(lp[2:3] * lp[3:4], axis=-1, keepdims=True)) + lambda_init)
    lane = lax.broadcasted_iota(jnp.int32, (TQ, DA_V_DIM), 1)
    krow = lax.broadcasted_iota(jnp.int32, (TK, 2 * TQ), 0) // CHUNK
    qcol = (lax.broadcasted_iota(jnp.int32, (TK, 2 * TQ), 1) % TQ) // CHUNK
    diag_keep = krow <= qcol

    def scores(qq, kstart):
        kblk = k_ref[0, pl.ds(kstart, TK), :]
        return lax.dot_general(kblk, qq, (((1,), (1,)), ((), ())), preferred_element_type=F32)

    ones_rows = jnp.ones((2 * SUBLANES, TK), BF16)

    def block(st, s, s_max, kstart, m):
        vtblk = jnp.concatenate([vt_ref[0, :, pl.ds(kstart, TK)], ones_rows], axis=0)
        m_new = jnp.maximum(m, s_max)
        alpha = jnp.exp2(m - m_new)
        p = jnp.exp2(s - m_new)
        acc_ref[st] = alpha * acc_ref[st] + jnp.dot(vtblk, p.astype(BF16), preferred_element_type=F32)
        return m_new

    def masked_block(st, s, kstart, m):
        s = jnp.where(diag_keep, s, NEG_BIG)
        return block(st, s, jnp.max(s, axis=0, keepdims=True), kstart, m)

    def finish(st):
        acc = acc_ref[st]
        o = acc[:DA_V_DIM] / acc[DA_V_DIM:DA_V_DIM + 1]
        o = o[:, :TQ] - lam * o[:, TQ:]
        o = o * lax.rsqrt(jnp.mean(o * o, axis=0, keepdims=True) + SUBLN_EPS)
        return (o * subln_ref[...] * (1.0 - lambda_init)).T.astype(BF16)

    def q_tile(t, _):
        row0 = pl.multiple_of(t * 2 * TQ, 2 * TQ)
        qqs = []
        for st in range(2):
            q = q_ref[0, pl.ds(row0 + st * TQ, TQ), :]
            zero = jnp.zeros_like(q)
            qqs.append(jnp.concatenate([jnp.where(lane < DA_HEAD_DIM, q, zero),
                                        jnp.where(lane >= DA_HEAD_DIM, q, zero)], axis=0))
        acc_ref[...] = jnp.zeros_like(acc_ref)
        m0 = jnp.full((1, 2 * TQ), NEG_BIG, F32)

        smax0 = []
        for st in range(2):
            s0 = scores(qqs[st], 0)
            s_ref[0, st] = s0
            smax0.append(jnp.max(s0, axis=0, keepdims=True))

        def one_block(kstart, cur, carry):
            ms, smax = carry
            s_cur = [s_ref[cur, st] for st in range(2)]
            smax_next = []
            for st in range(2):
                s_next = scores(qqs[st], kstart + TK)
                s_ref[1 - cur, st] = s_next
                smax_next.append(jnp.max(s_next, axis=0, keepdims=True))
            ms = tuple(block(st, s_cur[st], smax[st], kstart, ms[st]) for st in range(2))
            return ms, tuple(smax_next)

        def full_blocks(kp, carry):
            kstart = pl.multiple_of(kp * 2 * TK, 2 * TK)
            return one_block(kstart + TK, 1, one_block(kstart, 0, carry))

        (ma, mb), (_, smax_b) = lax.fori_loop(0, t, full_blocks, ((m0, m0), tuple(smax0)))
        s_last = scores(qqs[1], row0 + TK)
        masked_block(0, s_ref[0, 0], row0, ma)
        mb = block(1, s_ref[0, 1], smax_b, row0, mb)
        masked_block(1, s_last, row0 + TK, mb)
        o_ref[0, pl.ds(row0, TQ), :] = finish(0)
        o_ref[0, pl.ds(row0 + TQ, TQ), :] = finish(1)
        return 0

    lax.fori_loop(0, S // (2 * TQ), q_tile, 0)


def _attention_layer(h, positions, gn, w_qkv, lam_p, subln, layer):
    B, S, D = h.shape
    n_heads = w_qkv.shape[1] // (3 * DA_V_DIM)
    nq = n_heads * 2 * DA_HEAD_DIM
    T = SEQ_TILE
    assert S % T == 0 and S % ATTN_TQ == 0 and w_qkv.shape[1] == 3 * nq
    half = ROPE_DIM // 2
    inv_freq = jnp.power(ROPE_THETA, -jnp.arange(half, dtype=F32) * 2.0 / ROPE_DIM)
    d = jnp.arange(LANES) % DA_HEAD_DIM
    freq_tab = jnp.stack([inv_freq[d % half], (d < half).astype(F32),
                          ((d >= half) & (d < ROPE_DIM)).astype(F32)]
                         + [jnp.zeros((LANES,), F32)] * 5)
    w_qk = w_qkv[:, :2 * nq].astype(BF16)
    w_vt = w_qkv[:, 2 * nq:].T.astype(BF16)
    q, k, vt = pl.pallas_call(
        _attn_qkv_kernel,
        grid=(B, S // T),
        in_specs=[
            pl.BlockSpec((1, T, D), lambda b, s: (b, s, 0)),
            pl.BlockSpec((1, T, 1), lambda b, s: (b, s, 0)),
            _const_spec((1, D)),
            _const_spec((D, 2 * nq)),
            _const_spec((nq, D)),
            _const_spec((SUBLANES, LANES)),
        ],
        out_specs=[pl.BlockSpec((1, T, nq), lambda b, s: (b, s, 0)),
                   pl.BlockSpec((1, T, nq), lambda b, s: (b, s, 0)),
                   pl.BlockSpec((1, nq, T), lambda b, s: (b, 0, s))],
        out_shape=[jax.ShapeDtypeStruct((B, S, nq), BF16), jax.ShapeDtypeStruct((B, S, nq), BF16),
                   jax.ShapeDtypeStruct((B, nq, S), BF16)],
        compiler_params=_params("parallel", "parallel"),
        name="attn_qkv",
    )(h, positions.reshape(B, S, 1), gn.reshape(1, D), w_qk, w_vt, freq_tab)

    kern = functools.partial(_attn_core_kernel, lambda_init=_lambda_init(layer))
    return pl.pallas_call(
        kern,
        grid=(B, n_heads),
        in_specs=[
            pl.BlockSpec((1, S, DA_V_DIM), lambda b, hd: (b, 0, hd)),
            pl.BlockSpec((1, S, DA_V_DIM), lambda b, hd: (b, 0, hd)),
            pl.BlockSpec((1, DA_V_DIM, S), lambda b, hd: (b, hd, 0)),
            _const_spec((4, DA_HEAD_DIM)),
            _const_spec((DA_V_DIM, 1)),
        ],
        out_specs=pl.BlockSpec((1, S, DA_V_DIM), lambda b, hd: (b, 0, hd)),
        out_shape=jax.ShapeDtypeStruct((B, S, nq), BF16),
        scratch_shapes=[pltpu.VMEM((2, 2, ATTN_TK, 2 * ATTN_TQ), F32),
                        pltpu.VMEM((2, DA_V_DIM + 2 * SUBLANES, 2 * ATTN_TQ), F32)],
        compiler_params=_params("parallel", "parallel"),
        name="attn_core",
    )(q, k, vt, lam_p, subln.reshape(DA_V_DIM, 1))


def _conv_mix_kernel(h_ref, gn_ref, win_ref, bin_ref, dw_ref, lng_ref, lnb_ref, o_ref, u_ref, *, n_taps):
    T = h_ref.shape[1]
    D = h_ref.shape[2]
    HALO = 4 * SUBLANES
    RB = 64
    assert n_taps - 1 <= HALO and T % RB == 0

    @pl.when(pl.program_id(1) == 0)
    def _():
        u_ref[0:HALO, :] = jnp.zeros((HALO, D), F32)

    hn = (_rms(h_ref[0], RMS_EPS) * gn_ref[...]).astype(BF16)
    a = jnp.dot(hn, win_ref[:, 0:D], preferred_element_type=F32) + bin_ref[:, 0:D]
    g = jnp.dot(hn, win_ref[:, D:2 * D], preferred_element_type=F32) + bin_ref[:, D:2 * D]
    u_ref[HALO:HALO + T, :] = a * _sigmoid(g)

    CB = 2 * LANES
    groups = {}
    for t in range(n_taps):
        off = HALO - (n_taps - 1) + t
        groups.setdefault(off % SUBLANES, []).append((t, off // SUBLANES))
    for rb in range(T // RB):
        cols = []
        for cb in range(D // CB):
            cs = slice(cb * CB, (cb + 1) * CB)
            y = jnp.zeros((RB, CB), F32) + dw_ref[n_taps:n_taps + 1, cs]
            for r, taps in groups.items():
                n_rows = RB + SUBLANES * max(a for _, a in taps)
                if r == 0:
                    win = u_ref[rb * RB:rb * RB + n_rows, cs]
                else:
                    span = n_rows + SUBLANES
                    win = pltpu.roll(u_ref[rb * RB:rb * RB + span, cs], span - r, 0)[0:n_rows]
                for t, a in taps:
                    y = y + win[SUBLANES * a:SUBLANES * a + RB] * dw_ref[t:t + 1, cs]
            cols.append(y)
        y = jnp.concatenate(cols, axis=1)
        mu = jnp.mean(y, axis=-1, keepdims=True)
        yc = y - mu
        var = jnp.mean(yc * yc, axis=-1, keepdims=True)
        z = yc * lax.rsqrt(var + LN_EPS) * lng_ref[...] + lnb_ref[...]
        o_ref[0, rb * RB:(rb + 1) * RB, :] = (z * _sigmoid(z)).astype(BF16)
    u_ref[0:HALO, :] = u_ref[T:T + HALO, :]


def _conformer_layer(h, gn, w_in, b_in, dw, dw_b, ln_g, ln_b):
    B, S, D = h.shape
    T = SEQ_TILE
    n_taps = dw.shape[0]
    dwb = jnp.concatenate([dw, dw_b[None]], axis=0)
    kern = functools.partial(_conv_mix_kernel, n_taps=n_taps)
    return pl.pallas_call(
        kern,
        grid=(B, S // T),
        in_specs=[
            pl.BlockSpec((1, T, D), lambda b, s: (b, s, 0)),
            _const_spec((1, D)),
            _const_spec((D, 2 * D)),
            _const_spec((1, 2 * D)),
            _const_spec((n_taps + 1, D)),
            _const_spec((1, D)),
            _const_spec((1, D)),
        ],
        out_specs=pl.BlockSpec((1, T, D), lambda b, s: (b, s, 0)),
        out_shape=jax.ShapeDtypeStruct((B, S, D), BF16),
        scratch_shapes=[pltpu.VMEM((4 * SUBLANES + T, D), F32)],
        compiler_params=_params("parallel", "arbitrary"),
        name="conv_mix",
    )(h, gn.reshape(1, D), w_in.astype(BF16), b_in.reshape(1, 2 * D), dwb, ln_g.reshape(1, D),
      ln_b.reshape(1, D))


def _gdn_proj_kernel(h_ref, gn_ref, wqkv_ref, cw_ref, wz_ref, wba_ref, gate_ref,
                     q_ref, k_ref, v_ref, z_ref, bg_ref, carry_ref, u_ref, *, n_taps, key_dim, n_vheads):
    T = h_ref.shape[1]
    H = SUBLANES
    CW = 4 * LANES
    n_qkv = wqkv_ref.shape[1]
    n_chunks = n_qkv // CW
    assert n_taps - 1 <= H and key_dim % CW == 0

    @pl.when(pl.program_id(1) == 0)
    def _():
        carry_ref[...] = jnp.zeros_like(carry_ref)

    hn = (_rms(h_ref[0], RMS_EPS) * gn_ref[...]).astype(BF16)
    u_next = jnp.dot(hn, wqkv_ref[:, 0:CW], preferred_element_type=F32)
    for c in range(n_chunks):
        slot = c % 2
        u = u_next
        if c + 1 < n_chunks:
            u_next = jnp.dot(hn, wqkv_ref[:, (c + 1) * CW:(c + 2) * CW], preferred_element_type=F32)
        u_ref[slot, 0:H, :] = carry_ref[c]
        u_ref[slot, H:H + T, :] = u
        cw = cw_ref[:, c * CW:(c + 1) * CW]
        y = u * cw[n_taps - 1:n_taps]
        for t in range(n_taps - 1):
            off = H - (n_taps - 1) + t
            y = y + u_ref[slot, off:off + T, :] * cw[t:t + 1]
        carry_ref[c] = u[T - H:T, :]
        y = y * _sigmoid(y)
        col = c * CW
        for j in range(CW // LANES):
            yj = y[:, j * LANES:(j + 1) * LANES]
            cj = col + j * LANES
            if cj < 2 * key_dim:
                yj = yj * lax.rsqrt(jnp.sum(yj * yj, axis=-1, keepdims=True) + L2_EPS)
                if cj < key_dim:
                    q_ref[0, :, cj:cj + LANES] = (yj * (GDN_HEAD_DIM ** -0.5)).astype(BF16)
                else:
                    k_ref[0, :, cj - key_dim:cj - key_dim + LANES] = yj.astype(BF16)
            else:
                v_ref[0, :, cj - 2 * key_dim:cj - 2 * key_dim + LANES] = yj.astype(BF16)
    z_ref[0] = jnp.dot(hn, wz_ref[...], preferred_element_type=F32).astype(BF16)
    ba = jnp.dot(hn, wba_ref[...], preferred_element_type=F32)
    is_beta = gate_ref[0:1, :]
    is_g = gate_ref[1:2, :]
    xs = ba + gate_ref[3:4, :]
    softplus = jnp.maximum(xs, 0.0) + jnp.log(1.0 + jnp.exp(-jnp.abs(xs)))
    g = -is_g * jnp.exp(gate_ref[2:3, :]) * softplus
    R = 4 * CHUNK
    ri = lax.broadcasted_iota(jnp.int32, (R, R), 0)
    ci = lax.broadcasted_iota(jnp.int32, (R, R), 1)
    tril = (((ri // CHUNK) == (ci // CHUNK)) & (ci <= ri)).astype(BF16)
    parts = []
    for i in range(T // R):
        gi = g[i * R:(i + 1) * R]
        hi = gi.astype(BF16)
        r1 = gi - hi.astype(F32)
        mid = r1.astype(BF16)
        lo = (r1 - mid.astype(F32)).astype(BF16)
        both = jnp.dot(tril, jnp.concatenate([hi, mid], axis=1), preferred_element_type=F32)
        parts.append(both[:, :LANES] + both[:, LANES:] + jnp.dot(tril, lo, preferred_element_type=F32))
    gc = jnp.concatenate(parts, axis=0)
    gc3 = gc.reshape(T // CHUNK, CHUNK, LANES)
    glast = jnp.broadcast_to(gc3[:, CHUNK - 1:CHUNK, :], gc3.shape).reshape(T, LANES)
    bg_ref[0] = is_beta * _sigmoid(ba) + gc + pltpu.roll(glast, n_vheads, 1)


def _split(a):
    hi = a.astype(BF16)
    return hi, (a - hi.astype(F32)).astype(BF16)


def _mm3(ah, al, bh, bl):
    d = functools.partial(jnp.dot, preferred_element_type=F32)
    return d(jnp.concatenate([ah, al], axis=1), jnp.concatenate([bh, bh], axis=0)) + d(ah, bl)


def _gdn_core_kernel(q_ref, k_ref, kt_ref, v_ref, z_ref, col_ref, row_ref, norm_ref, o_ref, state_ref,
                     *, heads_per_step):
    S = q_ref.shape[1]
    G = GDN_GROUP
    C = CHUNK
    P = 2 * C
    NP = G // P
    HB = heads_per_step
    rep = 2
    DH = GDN_HEAD_DIM
    n_rounds = int(math.log2(C)) - 1

    state_ref[...] = jnp.zeros_like(state_ref)
    ri = lax.broadcasted_iota(jnp.int32, (P, P), 0)
    ci = lax.broadcasted_iota(jnp.int32, (P, P), 1)
    same = (ri // C) == (ci // C)
    tril = same & (ci <= ri)
    strict_f = (same & (ci < ri)).astype(F32)
    eye_f = (ri == ci).astype(F32)
    lane_c = lax.broadcasted_iota(jnp.int32, (1, P), 1) // C

    def group(gi, _):
        r0 = pl.multiple_of(gi * G, G)
        probs = [(hl, pi) for hl in range(HB) for pi in range(NP)]
        x_l, aqk_l, rhs_l, qe_l, ktd_l, elast_l = {}, {}, {}, {}, {}, {}
        for kh in range(HB // rep):
            ksl = slice(kh * DH, (kh + 1) * DH)
            for pi in range(NP):
                rp = r0 + pi * P
                kg = k_ref[0, pl.ds(rp, P), ksl]
                qg = q_ref[0, pl.ds(rp, P), ksl]
                ktg = kt_ref[0, ksl, pl.ds(rp, P)]
                kk = jnp.dot(kg, ktg, preferred_element_type=F32)
                qk = jnp.dot(qg, ktg, preferred_element_type=F32)
                cols = col_ref[0, 0, pl.ds(rp, P), :]
                rows = row_ref[0, 0, :, pl.ds(rp, P)]
                for r in range(rep):
                    hl = kh * rep + r
                    beta_c = cols[:, hl:hl + 1]
                    gc_c = cols[:, HB + hl:HB + hl + 1]
                    glast_c = cols[:, 2 * HB + hl:2 * HB + hl + 1]
                    gc_r = rows[HB + hl:HB + hl + 1, :]
                    glast_r = rows[2 * HB + hl:2 * HB + hl + 1, :]
                    decay = jnp.exp(jnp.where(tril, gc_c - gc_r, NEG_BIG))
                    x_l[hl, pi] = -(kk * beta_c * decay * strict_f)
                    aqk_l[hl, pi] = (qk * decay).astype(BF16)
                    egc = jnp.exp(gc_c)
                    vg = v_ref[0, pl.ds(rp, P), hl * DH:(hl + 1) * DH].astype(F32)
                    rhs_l[hl, pi] = jnp.concatenate([vg * beta_c, kg.astype(F32) * (beta_c * egc)],
                                                    axis=1).astype(BF16)
                    qe_l[hl, pi] = qg.astype(F32) * egc
                    ktd = ktg.astype(F32) * jnp.exp(glast_r - gc_r)
                    ktd_l[hl, pi] = [jnp.where(lane_c == c, ktd, 0.0).astype(BF16) for c in range(2)]
                    elast_l[hl, pi] = [jnp.exp(glast_c[c * C:c * C + 1, :]) for c in range(2)]
        p_l, y_l = {}, {}
        for pr in probs:
            xh, xl = _split(x_l[pr])
            p_l[pr] = eye_f + x_l[pr]
            y_l[pr] = _mm3(xh, xl, xh, xl)
        for it in range(n_rounds):
            for pr in probs:
                yh, yl = _split(y_l[pr])
                ph, pl_ = _split(p_l[pr])
                if it < n_rounds - 1:
                    both = _mm3(yh, yl, jnp.concatenate([ph, yh], axis=1), jnp.concatenate([pl_, yl], axis=1))
                    p_l[pr] = p_l[pr] + both[:, :P]
                    y_l[pr] = both[:, P:]
                else:
                    p_l[pr] = p_l[pr] + _mm3(yh, yl, ph, pl_)
        an_l, ob_l = {}, {}
        for pr in probs:
            uw = jnp.dot(p_l[pr].astype(BF16), rhs_l[pr], preferred_element_type=F32).astype(BF16)
            ob_l[pr] = jnp.dot(aqk_l[pr], uw, preferred_element_type=F32)
            an_l[pr] = [jnp.dot(ktd_l[pr][c], uw, preferred_element_type=F32) for c in range(2)]
        sts = [state_ref[hl] for hl in range(HB)]
        o_l = {hl: [] for hl in range(HB)}
        for pi in range(NP):
            for c in range(2):
                for hl in range(HB):
                    pr = (hl, pi)
                    rs = slice(c * C, (c + 1) * C)
                    an = an_l[pr][c]
                    qp = qe_l[pr][rs] - ob_l[pr][rs, DH:]
                    lhs = jnp.concatenate([an[:, DH:], qp], axis=0).astype(BF16)
                    res = jnp.dot(lhs, sts[hl].astype(BF16), preferred_element_type=F32)
                    o_l[hl].append(ob_l[pr][rs, :DH] + res[DH:])
                    sts[hl] = sts[hl] * elast_l[pr][c] + an[:, :DH] - res[:DH]
        for hl in range(HB):
            state_ref[hl] = sts[hl]
            vsl = slice(hl * DH, (hl + 1) * DH)
            o = jnp.concatenate(o_l[hl], axis=0)
            zg = z_ref[0, pl.ds(r0, G), vsl].astype(F32)
            o = _rms(o, RMS_EPS) * norm_ref[...] * (zg * _sigmoid(zg))
            o_ref[0, pl.ds(r0, G), vsl] = o.astype(BF16)
        return 0

    lax.fori_loop(0, S // G, group, 0)


def _gdn_layer(h, gn, w_in, conv_w, a_log, dt_bias, norm_g):
    B, S, D = h.shape
    T = SEQ_TILE
    n_vheads = a_log.shape[0]
    val_dim = n_vheads * GDN_HEAD_DIM
    key_dim = val_dim // 2
    n_qkv = 2 * key_dim + val_dim
    n_taps = conv_w.shape[0]
    assert w_in.shape[1] == n_qkv + val_dim + 2 * n_vheads and 3 * n_vheads <= LANES
    w_qkv = w_in[:, :n_qkv].astype(BF16)
    w_z = w_in[:, n_qkv:n_qkv + val_dim].astype(BF16)
    w_ba = jnp.pad(w_in[:, n_qkv + val_dim:], ((0, 0), (0, LANES - 2 * n_vheads))).astype(BF16)
    lane = jnp.arange(LANES)
    zeros_h = jnp.zeros((LANES - 2 * n_vheads,), F32)
    gate_tab = jnp.stack([
        (lane < n_vheads).astype(F32),
        ((lane >= n_vheads) & (lane < 2 * n_vheads)).astype(F32),
        jnp.concatenate([jnp.zeros((n_vheads,), F32), a_log.astype(F32), zeros_h]),
        jnp.concatenate([jnp.zeros((n_vheads,), F32), dt_bias.astype(F32), zeros_h]),
    ] + [jnp.zeros((LANES,), F32)] * 4)
    CW = 4 * LANES
    kern = functools.partial(_gdn_proj_kernel, n_taps=n_taps, key_dim=key_dim, n_vheads=n_vheads)
    q, k, v, z, bg = pl.pallas_call(
        kern,
        grid=(B, S // T),
        in_specs=[
            pl.BlockSpec((1, T, D), lambda b, s: (b, s, 0)),
            _const_spec((1, D)),
            _const_spec((D, n_qkv)),
            _const_spec((n_taps, n_qkv)),
            _const_spec((D, val_dim)),
            _const_spec((D, LANES)),
            _const_spec((SUBLANES, LANES)),
        ],
        out_specs=[
            pl.BlockSpec((1, T, key_dim), lambda b, s: (b, s, 0)),
            pl.BlockSpec((1, T, key_dim), lambda b, s: (b, s, 0)),
            pl.BlockSpec((1, T, val_dim), lambda b, s: (b, s, 0)),
            pl.BlockSpec((1, T, val_dim), lambda b, s: (b, s, 0)),
            pl.BlockSpec((1, T, LANES), lambda b, s: (b, s, 0)),
        ],
        out_shape=[
            jax.ShapeDtypeStruct((B, S, key_dim), BF16),
            jax.ShapeDtypeStruct((B, S, key_dim), BF16),
            jax.ShapeDtypeStruct((B, S, val_dim), BF16),
            jax.ShapeDtypeStruct((B, S, val_dim), BF16),
            jax.ShapeDtypeStruct((B, S, LANES), F32),
        ],
        scratch_shapes=[pltpu.VMEM((n_qkv // CW, SUBLANES, CW), F32),
                        pltpu.VMEM((2, SUBLANES + T, CW), F32)],
        compiler_params=_params("parallel", "arbitrary"),
        name="gdn_proj",
    )(h, gn.reshape(1, D), w_qkv, conv_w, w_z, w_ba, gate_tab)

    HB = GDN_HEADS_PER_STEP
    KB = HB // 2 * GDN_HEAD_DIM
    VB = HB * GDN_HEAD_DIM
    NG = n_vheads // HB
    assert n_vheads % HB == 0 and S % GDN_GROUP == 0
    kt = jnp.swapaxes(k, 1, 2)
    gate_cols = jnp.concatenate([bg[:, :, i * n_vheads:(i + 1) * n_vheads].reshape(B, S, NG, HB)
                                 for i in range(3)], axis=-1)
    gate_cols = gate_cols.transpose(0, 2, 1, 3)
    gate_rows = jnp.swapaxes(gate_cols, 2, 3)
    kern = functools.partial(_gdn_core_kernel, heads_per_step=HB)
    return pl.pallas_call(
        kern,
        grid=(B, NG),
        in_specs=[
            pl.BlockSpec((1, S, KB), lambda b, g: (b, 0, g)),
            pl.BlockSpec((1, S, KB), lambda b, g: (b, 0, g)),
            pl.BlockSpec((1, KB, S), lambda b, g: (b, g, 0)),
            pl.BlockSpec((1, S, VB), lambda b, g: (b, 0, g)),
            pl.BlockSpec((1, S, VB), lambda b, g: (b, 0, g)),
            pl.BlockSpec((1, 1, S, 3 * HB), lambda b, g: (b, g, 0, 0)),
            pl.BlockSpec((1, 1, 3 * HB, S), lambda b, g: (b, g, 0, 0)),
            _const_spec((1, GDN_HEAD_DIM)),
        ],
        out_specs=pl.BlockSpec((1, S, VB), lambda b, g: (b, 0, g)),
        out_shape=jax.ShapeDtypeStruct((B, S, val_dim), BF16),
        scratch_shapes=[pltpu.VMEM((HB, GDN_HEAD_DIM, GDN_HEAD_DIM), F32)],
        compiler_params=_params("parallel", "parallel"),
        name="gdn_core",
    )(q, k, kt, v, z, gate_cols, gate_rows, norm_g.reshape(1, GDN_HEAD_DIM))


def kernel(x, positions, norm_mix, norm_ffn, norm_final, attn_w_qkv, attn_lambda, attn_subln, attn_w_o,
           conv_w_in, conv_b_in, conv_dw, conv_dw_b, conv_ln_g, conv_ln_b, conv_w_out, conv_b_out,
           gdn_w_in, gdn_conv, gdn_a_log, gdn_dt_bias, gdn_norm, gdn_w_o, ffn_w_in, ffn_dw, ffn_dw_b,
           ffn_w_out):
    depth = norm_mix.shape[0]
    d_model = x.shape[-1]
    no_bias = jnp.zeros((d_model,), F32)
    h = x
    for i in range(depth):
        j = i // N_MIXERS
        kind = i % N_MIXERS
        if kind == 0:
            mix = _attention_layer(h, positions, norm_mix[i], attn_w_qkv[j], attn_lambda[j],
                                   attn_subln[j], i)
            w_o, b_o = attn_w_o[j], no_bias
        elif kind == 1:
            mix = _conformer_layer(h, norm_mix[i], conv_w_in[j], conv_b_in[j], conv_dw[j], conv_dw_b[j],
                                   conv_ln_g[j], conv_ln_b[j])
            w_o, b_o = conv_w_out[j], conv_b_out[j]
        else:
            mix = _gdn_layer(h, norm_mix[i], gdn_w_in[j], gdn_conv[j], gdn_a_log[j], gdn_dt_bias[j],
                             gdn_norm[j])
            w_o, b_o = gdn_w_o[j], no_bias
        h = _ffn_layer(h, mix, w_o, b_o, norm_ffn[i], ffn_w_in[i], ffn_dw[i], ffn_dw_b[i], ffn_w_out[i],
                       norm_final, final_norm=(i == depth - 1))
    return h
```

```python
import functools
import math

import jax
import jax.numpy as jnp
from jax import lax
from jax.experimental import pallas as pl
from jax.experimental.pallas import tpu as pltpu

F32 = jnp.float32
BF16 = jnp.bfloat16

N_MIXERS = 3
RMS_EPS = 1e-6
CHUNK = 64
DA_HEAD_DIM = 64
DA_V_DIM = 2 * DA_HEAD_DIM
ROPE_THETA = 500000.0
ROPE_DIM = DA_HEAD_DIM // 4
SUBLN_EPS = 1e-5
LN_EPS = 1e-5
GDN_HEAD_DIM = 128
L2_EPS = 1e-6

LANES = 128
SUBLANES = 8
VMEM_LIMIT_BYTES = 56 * 1024 * 1024

SEQ_TILE = 512
ATTN_TQ = 512
ATTN_TK = 512
GDN_GROUP = 4 * CHUNK
GDN_HEADS_PER_STEP = 4
NEG_BIG = -1e30
LOG2_E = math.log2(math.e)


def _lambda_init(layer):
    return 0.8 - 0.6 * math.exp(-0.3 * layer)


def _rms(x, eps):
    return x * lax.rsqrt(jnp.mean(x * x, axis=-1, keepdims=True) + eps)


def _sigmoid(x):
    return 1.0 / (1.0 + jnp.exp(-x))


def _bdot(a, b):
    return jnp.dot(a.astype(BF16), b.astype(BF16), preferred_element_type=F32)


def _fdot(a, b):
    return jnp.dot(a, b, preferred_element_type=F32, precision=lax.Precision.HIGHEST)


def _const_spec(shape):
    nd = len(shape)
    return pl.BlockSpec(shape, lambda *_: (0,) * nd, pipeline_mode=pl.Buffered(1))


def _params(*sem):
    return pltpu.CompilerParams(dimension_semantics=sem, vmem_limit_bytes=VMEM_LIMIT_BYTES)


def _ffn_kernel(h_ref, x_ref, wx_ref, bx_ref, gn_ref, win_ref, dw_ref, wout_ref, gfin_ref, o_ref,
                carry_ref, act_ref, *, n_chunks, fc, final_norm):
    T = h_ref.shape[1]
    H = SUBLANES

    @pl.when(pl.program_id(1) == 0)
    def _():
        carry_ref[...] = jnp.zeros_like(carry_ref)

    first_row = lax.broadcasted_iota(jnp.int32, (H, 1), 0) == 0

    def shift_rows(a, prev_tail):
        r = pltpu.roll(a, 1, 0)
        return jnp.concatenate([jnp.where(first_row, prev_tail[H - 1:H, :], r[0:H]), r[H:]], axis=0)

    h1 = h_ref[0] + jnp.dot(x_ref[0], wx_ref[...], preferred_element_type=F32) + bx_ref[...]
    hn = (_rms(h1, RMS_EPS) * gn_ref[...]).astype(BF16)
    u_next = jnp.dot(hn, win_ref[0], preferred_element_type=F32)
    for c in range(n_chunks):
        u = u_next
        if c + 1 < n_chunks:
            u_next = jnp.dot(hn, win_ref[c + 1], preferred_element_type=F32)
        dw = dw_ref[c]
        z = u * dw[1:2] + shift_rows(u, carry_ref[c, 0]) * dw[0:1]
        y = u * dw[2:3] + shift_rows(z, carry_ref[c, 1]) + dw[3:4]
        carry_ref[c, 0] = u[T - H:T, :]
        carry_ref[c, 1] = z[T - H:T, :]
        g = y[:, :fc]
        act_ref[:, c * fc:(c + 1) * fc] = (g * _sigmoid(g) * y[:, fc:]).astype(BF16)
    acc = h1 + jnp.dot(act_ref[...], wout_ref[...], preferred_element_type=F32)
    if final_norm:
        acc = _rms(acc, RMS_EPS) * gfin_ref[...]
    o_ref[0] = acc


def _ffn_layer(h, x, wx, bx, gn, w_in, dw, dw_b, w_out, gfin, *, final_norm):
    B, S, D = h.shape
    Kx = x.shape[-1]
    d_ff = w_out.shape[0]
    fc = 2 * LANES
    assert d_ff % fc == 0 and S % SEQ_TILE == 0
    n_chunks = d_ff // fc
    T = SEQ_TILE
    w_in_c = jnp.concatenate([w_in[:, :d_ff].reshape(D, n_chunks, fc),
                              w_in[:, d_ff:].reshape(D, n_chunks, fc)], axis=-1)
    w_in_c = w_in_c.transpose(1, 0, 2).astype(BF16)
    dwb = jnp.concatenate([dw, dw_b[None]], axis=0)
    dwb_c = jnp.concatenate([dwb[:, :d_ff].reshape(4, n_chunks, fc),
                             dwb[:, d_ff:].reshape(4, n_chunks, fc)], axis=-1).transpose(1, 0, 2)
    w_out_c = w_out.astype(BF16)
    kern = functools.partial(_ffn_kernel, n_chunks=n_chunks, fc=fc, final_norm=final_norm)
    return pl.pallas_call(
        kern,
        grid=(B, S // T),
        in_specs=[
            pl.BlockSpec((1, T, D), lambda b, s: (b, s, 0)),
            pl.BlockSpec((1, T, Kx), lambda b, s: (b, s, 0)),
            _const_spec((Kx, D)),
            _const_spec((1, D)),
            _const_spec((1, D)),
            _const_spec((n_chunks, D, 2 * fc)),
            _const_spec((n_chunks, 4, 2 * fc)),
            _const_spec((d_ff, D)),
            _const_spec((1, D)),
        ],
        out_specs=pl.BlockSpec((1, T, D), lambda b, s: (b, s, 0)),
        out_shape=jax.ShapeDtypeStruct((B, S, D), F32),
        scratch_shapes=[pltpu.VMEM((n_chunks, 2, SUBLANES, 2 * fc), F32),
                        pltpu.VMEM((T, d_ff), BF16)],
        compiler_params=_params("parallel", "arbitrary"),
        name="ffn",
    )(h, x, wx.astype(BF16), bx.reshape(1, D), gn.reshape(1, D), w_in_c, dwb_c, w_out_c,
      gfin.reshape(1, D))


def _attn_qkv_kernel(h_ref, pos_ref, gn_ref, w_ref, wvt_ref, freq_ref, q_ref, k_ref, vt_ref):
    nq = q_ref.shape[2]
    hn = (_rms(h_ref[0], RMS_EPS) * gn_ref[...]).astype(BF16)
    ang = pos_ref[0].astype(F32) * freq_ref[0:1, :]
    cos = jnp.cos(ang)
    sin = jnp.sin(ang)
    first = freq_ref[1:2, :]
    second = freq_ref[2:3, :]
    c_tab = (first + second) * cos + (1.0 - first - second)
    s_hi = -first * sin
    s_lo = second * sin

    def rope(x):
        return (x * c_tab + pltpu.roll(x, LANES - ROPE_DIM // 2, 1) * s_hi
                + pltpu.roll(x, ROPE_DIM // 2, 1) * s_lo)

    q = jnp.dot(hn, w_ref[:, 0:nq], preferred_element_type=F32)
    k = jnp.dot(hn, w_ref[:, nq:2 * nq], preferred_element_type=F32)
    for j in range(nq // LANES):
        sl = slice(j * LANES, (j + 1) * LANES)
        q_ref[0, :, sl] = (rope(q[:, sl]) * (DA_HEAD_DIM ** -0.5 * LOG2_E)).astype(BF16)
        k_ref[0, :, sl] = rope(k[:, sl]).astype(BF16)
    vt_ref[0] = lax.dot_general(wvt_ref[...], hn, (((1,), (1,)), ((), ())),
                                preferred_element_type=F32).astype(BF16)


def _attn_core_kernel(q_ref, k_ref, vt_ref, lam_ref, subln_ref, o_ref, s_ref, acc_ref, *, lambda_init):
    S = q_ref.shape[1]
    TQ = ATTN_TQ
    TK = ATTN_TK
    assert TQ == TK and S % (2 * TQ) == 0
    lp = lam_ref[...]
    lam = (jnp.exp(jnp.sum(lp[0:1] * lp[1:2], axis=-1, keepdims=True))
           - jnp.exp(jnp.sum(lp[2:3] * lp[3:4], axis=-1, keepdims=True)) + lambda_init)
    lane = lax.broadcasted_iota(jnp.int32, (TQ, DA_V_DIM), 1)
    krow = lax.broadcasted_iota(jnp.int32, (TK, 2 * TQ), 0) // CHUNK
    qcol = (lax.broadcasted_iota(jnp.int32, (TK, 2 * TQ), 1) % TQ) // CHUNK
    diag_keep = krow <= qcol

    def scores(qq, kstart):
        kblk = k_ref[0, pl.ds(kstart, TK), :]
        return lax.dot_general(kblk, qq, (((1,), (1,)), ((), ())), preferred_element_type=F32)

    ones_rows = jnp.ones((2 * SUBLANES, TK), BF16)

    def block(st, s, s_max, kstart, m):
        vtblk = jnp.concatenate([vt_ref[0, :, pl.ds(kstart, TK)], ones_rows], axis=0)
        m_new = jnp.maximum(m, s_max)
        alpha = jnp.exp2(m - m_new)
        p = jnp.exp2(s - m_new)
        acc_ref[st] = alpha * acc_ref[st] + jnp.dot(vtblk, p.astype(BF16), preferred_element_type=F32)
        return m_new

    def masked_block(st, s, kstart, m):
        s = jnp.where(diag_keep, s, NEG_BIG)
        return block(st, s, jnp.max(s, axis=0, keepdims=True), kstart, m)

    def finish(st):
        acc = acc_ref[st]
        o = acc[:DA_V_DIM] / acc[DA_V_DIM:DA_V_DIM + 1]
        o = o[:, :TQ] - lam * o[:, TQ:]
        o = o * lax.rsqrt(jnp.mean(o * o, axis=0, keepdims=True) + SUBLN_EPS)
        return (o * subln_ref[...] * (1.0 - lambda_init)).T.astype(BF16)

    def q_tile(t, _):
        row0 = pl.multiple_of(t * 2 * TQ, 2 * TQ)
        qqs = []
        for st in range(2):
            q = q_ref[0, pl.ds(row0 + st * TQ, TQ), :]
            zero = jnp.zeros_like(q)
            qqs.append(jnp.concatenate([jnp.where(lane < DA_HEAD_DIM, q, zero),
                                        jnp.where(lane >= DA_HEAD_DIM, q, zero)], axis=0))
        acc_ref[...] = jnp.zeros_like(acc_ref)
        m0 = jnp.full((1, 2 * TQ), NEG_BIG, F32)

        smax0 = []
        for st in range(2):
            s0 = scores(qqs[st], 0)
            s_ref[0, st] = s0
            smax0.append(jnp.max(s0, axis=0, keepdims=True))

        def one_block(kstart, cur, carry):
            ms, smax = carry
            s_cur = [s_ref[cur, st] for st in range(2)]
            smax_next = []
            for st in range(2):
                s_next = scores(qqs[st], kstart + TK)
                s_ref[1 - cur, st] = s_next
                smax_next.append(jnp.max(s_next, axis=0, keepdims=True))
            ms = tuple(block(st, s_cur[st], smax[st], kstart, ms[st]) for st in range(2))
            return ms, tuple(smax_next)

        def full_blocks(kp, carry):
            kstart = pl.multiple_of(kp * 2 * TK, 2 * TK)
            return one_block(kstart + TK, 1, one_block(kstart, 0, carry))

        (ma, mb), (_, smax_b) = lax.fori_loop(0, t, full_blocks, ((m0, m0), tuple(smax0)))
        s_last = scores(qqs[1], row0 + TK)
        masked_block(0, s_ref[0, 0], row0, ma)
        mb = block(1, s_ref[0, 1], smax_b, row0, mb)
        masked_block(1, s_last, row0 + TK, mb)
        o_ref[0, pl.ds(row0, TQ), :] = finish(0)
        o_ref[0, pl.ds(row0 + TQ, TQ), :] = finish(1)
        return 0

    lax.fori_loop(0, S // (2 * TQ), q_tile, 0)


def _attention_layer(h, positions, gn, w_qkv, lam_p, subln, layer):
    B, S, D = h.shape
    n_heads = w_qkv.shape[1] // (3 * DA_V_DIM)
    nq = n_heads * 2 * DA_HEAD_DIM
    T = SEQ_TILE
    assert S % T == 0 and S % ATTN_TQ == 0 and w_qkv.shape[1] == 3 * nq
    half = ROPE_DIM // 2
    inv_freq = jnp.power(ROPE_THETA, -jnp.arange(half, dtype=F32) * 2.0 / ROPE_DIM)
    d = jnp.arange(LANES) % DA_HEAD_DIM
    freq_tab = jnp.stack([inv_freq[d % half], (d < half).astype(F32),
                          ((d >= half) & (d < ROPE_DIM)).astype(F32)]
                         + [jnp.zeros((LANES,), F32)] * 5)
    w_qk = w_qkv[:, :2 * nq].astype(BF16)
    w_vt = w_qkv[:, 2 * nq:].T.astype(BF16)
    q, k, vt = pl.pallas_call(
        _attn_qkv_kernel,
        grid=(B, S // T),
        in_specs=[
            pl.BlockSpec((1, T, D), lambda b, s: (b, s, 0)),
            pl.BlockSpec((1, T, 1), lambda b, s: (b, s, 0)),
            _const_spec((1, D)),
            _const_spec((D, 2 * nq)),
            _const_spec((nq, D)),
            _const_spec((SUBLANES, LANES)),
        ],
        out_specs=[pl.BlockSpec((1, T, nq), lambda b, s: (b, s, 0)),
                   pl.BlockSpec((1, T, nq), lambda b, s: (b, s, 0)),
                   pl.BlockSpec((1, nq, T), lambda b, s: (b, 0, s))],
        out_shape=[jax.ShapeDtypeStruct((B, S, nq), BF16), jax.ShapeDtypeStruct((B, S, nq), BF16),
                   jax.ShapeDtypeStruct((B, nq, S), BF16)],
        compiler_params=_params("parallel", "parallel"),
        name="attn_qkv",
    )(h, positions.reshape(B, S, 1), gn.reshape(1, D), w_qk, w_vt, freq_tab)

    kern = functools.partial(_attn_core_kernel, lambda_init=_lambda_init(layer))
    return pl.pallas_call(
        kern,
        grid=(B, n_heads),
        in_specs=[
            pl.BlockSpec((1, S, DA_V_DIM), lambda b, hd: (b, 0, hd)),
            pl.BlockSpec((1, S, DA_V_DIM), lambda b, hd: (b, 0, hd)),
            pl.BlockSpec((1, DA_V_DIM, S), lambda b, hd: (b, hd, 0)),
            _const_spec((4, DA_HEAD_DIM)),
            _const_spec((DA_V_DIM, 1)),
        ],
        out_specs=pl.BlockSpec((1, S, DA_V_DIM), lambda b, hd: (b, 0, hd)),
        out_shape=jax.ShapeDtypeStruct((B, S, nq), BF16),
        scratch_shapes=[pltpu.VMEM((2, 2, ATTN_TK, 2 * ATTN_TQ), F32),
                        pltpu.VMEM((2, DA_V_DIM + 2 * SUBLANES, 2 * ATTN_TQ), F32)],
        compiler_params=_params("parallel", "parallel"),
        name="attn_core",
    )(q, k, vt, lam_p, subln.reshape(DA_V_DIM, 1))


def _conv_mix_kernel(h_ref, gn_ref, win_ref, bin_ref, dw_ref, lng_ref, lnb_ref, o_ref, u_ref, *, n_taps):
    T = h_ref.shape[1]
    D = h_ref.shape[2]
    HALO = 4 * SUBLANES
    RB = 128
    assert n_taps - 1 <= HALO and T % RB == 0

    @pl.when(pl.program_id(1) == 0)
    def _():
        u_ref[0:HALO, :] = jnp.zeros((HALO, D), F32)

    hn = (_rms(h_ref[0], RMS_EPS) * gn_ref[...]).astype(BF16)
    a = jnp.dot(hn, win_ref[:, 0:D], preferred_element_type=F32) + bin_ref[:, 0:D]
    g = jnp.dot(hn, win_ref[:, D:2 * D], preferred_element_type=F32) + bin_ref[:, D:2 * D]
    u_ref[HALO:HALO + T, :] = a * _sigmoid(g)

    CB = LANES
    groups = {}
    for t in range(n_taps):
        off = HALO - (n_taps - 1) + t
        groups.setdefault(off % SUBLANES, []).append((t, off // SUBLANES))
    for rb in range(T // RB):
        cols = []
        for cb in range(D // CB):
            cs = slice(cb * CB, (cb + 1) * CB)
            y = jnp.zeros((RB, CB), F32) + dw_ref[n_taps:n_taps + 1, cs]
            for r, taps in groups.items():
                n_rows = RB + SUBLANES * max(a for _, a in taps)
                if r == 0:
                    win = u_ref[rb * RB:rb * RB + n_rows, cs]
                else:
                    span = n_rows + SUBLANES
                    win = pltpu.roll(u_ref[rb * RB:rb * RB + span, cs], span - r, 0)[0:n_rows]
                for t, a in taps:
                    y = y + win[SUBLANES * a:SUBLANES * a + RB] * dw_ref[t:t + 1, cs]
            cols.append(y)
        y = jnp.concatenate(cols, axis=1)
        mu = jnp.mean(y, axis=-1, keepdims=True)
        yc = y - mu
        var = jnp.mean(yc * yc, axis=-1, keepdims=True)
        z = yc * lax.rsqrt(var + LN_EPS) * lng_ref[...] + lnb_ref[...]
        o_ref[0, rb * RB:(rb + 1) * RB, :] = (z * _sigmoid(z)).astype(BF16)
    u_ref[0:HALO, :] = u_ref[T:T + HALO, :]


def _conformer_layer(h, gn, w_in, b_in, dw, dw_b, ln_g, ln_b):
    B, S, D = h.shape
    T = SEQ_TILE
    n_taps = dw.shape[0]
    dwb = jnp.concatenate([dw, dw_b[None]], axis=0)
    kern = functools.partial(_conv_mix_kernel, n_taps=n_taps)
    return pl.pallas_call(
        kern,
        grid=(B, S // T),
        in_specs=[
            pl.BlockSpec((1, T, D), lambda b, s: (b, s, 0)),
            _const_spec((1, D)),
            _const_spec((D, 2 * D)),
            _const_spec((1, 2 * D)),
            _const_spec((n_taps + 1, D)),
            _const_spec((1, D)),
            _const_spec((1, D)),
        ],
        out_specs=pl.BlockSpec((1, T, D), lambda b, s: (b, s, 0)),
        out_shape=jax.ShapeDtypeStruct((B, S, D), BF16),
        scratch_shapes=[pltpu.VMEM((4 * SUBLANES + T, D), F32)],
        compiler_params=_params("parallel", "arbitrary"),
        name="conv_mix",
    )(h, gn.reshape(1, D), w_in.astype(BF16), b_in.reshape(1, 2 * D), dwb, ln_g.reshape(1, D),
      ln_b.reshape(1, D))


def _gdn_proj_kernel(h_ref, gn_ref, wqkv_ref, cw_ref, wz_ref, wba_ref, gate_ref,
                     q_ref, k_ref, v_ref, z_ref, bg_ref, carry_ref, u_ref, *, n_taps, key_dim, n_vheads):
    T = h_ref.shape[1]
    H = SUBLANES
    CW = 4 * LANES
    n_qkv = wqkv_ref.shape[1]
    n_chunks = n_qkv // CW
    assert n_taps - 1 <= H and key_dim % CW == 0

    @pl.when(pl.program_id(1) == 0)
    def _():
        carry_ref[...] = jnp.zeros_like(carry_ref)

    hn = (_rms(h_ref[0], RMS_EPS) * gn_ref[...]).astype(BF16)
    u_next = jnp.dot(hn, wqkv_ref[:, 0:CW], preferred_element_type=F32)
    for c in range(n_chunks):
        slot = c % 2
        u = u_next
        if c + 1 < n_chunks:
            u_next = jnp.dot(hn, wqkv_ref[:, (c + 1) * CW:(c + 2) * CW], preferred_element_type=F32)
        u_ref[slot, 0:H, :] = carry_ref[c]
        u_ref[slot, H:H + T, :] = u
        cw = cw_ref[:, c * CW:(c + 1) * CW]
        y = u * cw[n_taps - 1:n_taps]
        for t in range(n_taps - 1):
            off = H - (n_taps - 1) + t
            y = y + u_ref[slot, off:off + T, :] * cw[t:t + 1]
        carry_ref[c] = u[T - H:T, :]
        y = y * _sigmoid(y)
        col = c * CW
        for j in range(CW // LANES):
            yj = y[:, j * LANES:(j + 1) * LANES]
            cj = col + j * LANES
            if cj < 2 * key_dim:
                yj = yj * lax.rsqrt(jnp.sum(yj * yj, axis=-1, keepdims=True) + L2_EPS)
                if cj < key_dim:
                    q_ref[0, :, cj:cj + LANES] = (yj * (GDN_HEAD_DIM ** -0.5)).astype(BF16)
                else:
                    k_ref[0, :, cj - key_dim:cj - key_dim + LANES] = yj.astype(BF16)
            else:
                v_ref[0, :, cj - 2 * key_dim:cj - 2 * key_dim + LANES] = yj.astype(BF16)
    z_ref[0] = jnp.dot(hn, wz_ref[...], preferred_element_type=F32).astype(BF16)
    ba = jnp.dot(hn, wba_ref[...], preferred_element_type=F32)
    is_beta = gate_ref[0:1, :]
    is_g = gate_ref[1:2, :]
    xs = ba + gate_ref[3:4, :]
    softplus = jnp.maximum(xs, 0.0) + jnp.log(1.0 + jnp.exp(-jnp.abs(xs)))
    g = -is_g * jnp.exp(gate_ref[2:3, :]) * softplus
    R = 4 * CHUNK
    ri = lax.broadcasted_iota(jnp.int32, (R, R), 0)
    ci = lax.broadcasted_iota(jnp.int32, (R, R), 1)
    tril = (((ri // CHUNK) == (ci // CHUNK)) & (ci <= ri)).astype(BF16)
    parts = []
    for i in range(T // R):
        gi = g[i * R:(i + 1) * R]
        hi = gi.astype(BF16)
        r1 = gi - hi.astype(F32)
        mid = r1.astype(BF16)
        lo = (r1 - mid.astype(F32)).astype(BF16)
        both = jnp.dot(tril, jnp.concatenate([hi, mid], axis=1), preferred_element_type=F32)
        parts.append(both[:, :LANES] + both[:, LANES:] + jnp.dot(tril, lo, preferred_element_type=F32))
    gc = jnp.concatenate(parts, axis=0)
    gc3 = gc.reshape(T // CHUNK, CHUNK, LANES)
    glast = jnp.broadcast_to(gc3[:, CHUNK - 1:CHUNK, :], gc3.shape).reshape(T, LANES)
    bg_ref[0] = is_beta * _sigmoid(ba) + gc + pltpu.roll(glast, n_vheads, 1)


def _split(a):
    hi = a.astype(BF16)
    return hi, (a - hi.astype(F32)).astype(BF16)


def _mm3(ah, al, bh, bl):
    d = functools.partial(jnp.dot, preferred_element_type=F32)
    return d(jnp.concatenate([ah, al], axis=1), jnp.concatenate([bh, bh], axis=0)) + d(ah, bl)


def _gdn_core_kernel(q_ref, k_ref, kt_ref, v_ref, z_ref, col_ref, row_ref, norm_ref, o_ref, state_ref,
                     *, heads_per_step):
    S = q_ref.shape[1]
    G = GDN_GROUP
    C = CHUNK
    P = 2 * C
    NP = G // P
    HB = heads_per_step
    rep = 2
    DH = GDN_HEAD_DIM
    n_rounds = int(math.log2(C)) - 1

    state_ref[...] = jnp.zeros_like(state_ref)
    ri = lax.broadcasted_iota(jnp.int32, (P, P), 0)
    ci = lax.broadcasted_iota(jnp.int32, (P, P), 1)
    same = (ri // C) == (ci // C)
    tril = same & (ci <= ri)
    strict_f = (same & (ci < ri)).astype(F32)
    eye_f = (ri == ci).astype(F32)
    lane_c = lax.broadcasted_iota(jnp.int32, (1, P), 1) // C

    def group(gi, _):
        r0 = pl.multiple_of(gi * G, G)
        probs = [(hl, pi) for hl in range(HB) for pi in range(NP)]
        x_l, aqk_l, rhs_l, qe_l, ktd_l, elast_l = {}, {}, {}, {}, {}, {}
        for kh in range(HB // rep):
            ksl = slice(kh * DH, (kh + 1) * DH)
            for pi in range(NP):
                rp = r0 + pi * P
                kg = k_ref[0, pl.ds(rp, P), ksl]
                qg = q_ref[0, pl.ds(rp, P), ksl]
                ktg = kt_ref[0, ksl, pl.ds(rp, P)]
                kk = jnp.dot(kg, ktg, preferred_element_type=F32)
                qk = jnp.dot(qg, ktg, preferred_element_type=F32)
                cols = col_ref[0, 0, pl.ds(rp, P), :]
                rows = row_ref[0, 0, :, pl.ds(rp, P)]
                for r in range(rep):
                    hl = kh * rep + r
                    beta_c = cols[:, hl:hl + 1]
                    gc_c = cols[:, HB + hl:HB + hl + 1]
                    glast_c = cols[:, 2 * HB + hl:2 * HB + hl + 1]
                    gc_r = rows[HB + hl:HB + hl + 1, :]
                    glast_r = rows[2 * HB + hl:2 * HB + hl + 1, :]
                    decay = jnp.exp(jnp.where(tril, gc_c - gc_r, NEG_BIG))
                    x_l[hl, pi] = -(kk * beta_c * decay * strict_f)
                    aqk_l[hl, pi] = (qk * decay).astype(BF16)
                    egc = jnp.exp(gc_c)
                    vg = v_ref[0, pl.ds(rp, P), hl * DH:(hl + 1) * DH].astype(F32)
                    rhs_l[hl, pi] = jnp.concatenate([vg * beta_c, kg.astype(F32) * (beta_c * egc)],
                                                    axis=1).astype(BF16)
                    qe_l[hl, pi] = qg.astype(F32) * egc
                    ktd = ktg.astype(F32) * jnp.exp(glast_r - gc_r)
                    ktd_l[hl, pi] = [jnp.where(lane_c == c, ktd, 0.0).astype(BF16) for c in range(2)]
                    elast_l[hl, pi] = [jnp.exp(glast_c[c * C:c * C + 1, :]) for c in range(2)]
        p_l, y_l = {}, {}
        for pr in probs:
            xh, xl = _split(x_l[pr])
            p_l[pr] = eye_f + x_l[pr]
            y_l[pr] = _mm3(xh, xl, xh, xl)
        for it in range(n_rounds):
            for pr in probs:
                yh, yl = _split(y_l[pr])
                ph, pl_ = _split(p_l[pr])
                if it < n_rounds - 1:
                    both = _mm3(yh, yl, jnp.concatenate([ph, yh], axis=1), jnp.concatenate([pl_, yl], axis=1))
                    p_l[pr] = p_l[pr] + both[:, :P]
                    y_l[pr] = both[:, P:]
                else:
                    p_l[pr] = p_l[pr] + _mm3(yh, yl, ph, pl_)
        an_l, ob_l = {}, {}
        for pr in probs:
            uw = jnp.dot(p_l[pr].astype(BF16), rhs_l[pr], preferred_element_type=F32).astype(BF16)
            ob_l[pr] = jnp.dot(aqk_l[pr], uw, preferred_element_type=F32)
            an_l[pr] = [jnp.dot(ktd_l[pr][c], uw, preferred_element_type=F32) for c in range(2)]
        sts = [state_ref[hl] for hl in range(HB)]
        o_l = {hl: [] for hl in range(HB)}
        for pi in range(NP):
            for c in range(2):
                for hl in range(HB):
                    pr = (hl, pi)
                    rs = slice(c * C, (c + 1) * C)
                    an = an_l[pr][c]
                    qp = qe_l[pr][rs] - ob_l[pr][rs, DH:]
                    lhs = jnp.concatenate([an[:, DH:], qp], axis=0).astype(BF16)
                    res = jnp.dot(lhs, sts[hl].astype(BF16), preferred_element_type=F32)
                    o_l[hl].append(ob_l[pr][rs, :DH] + res[DH:])
                    sts[hl] = sts[hl] * elast_l[pr][c] + an[:, :DH] - res[:DH]
        for hl in range(HB):
            state_ref[hl] = sts[hl]
            vsl = slice(hl * DH, (hl + 1) * DH)
            o = jnp.concatenate(o_l[hl], axis=0)
            zg = z_ref[0, pl.ds(r0, G), vsl].astype(F32)
            o = _rms(o, RMS_EPS) * norm_ref[...] * (zg * _sigmoid(zg))
            o_ref[0, pl.ds(r0, G), vsl] = o.astype(BF16)
        return 0

    lax.fori_loop(0, S // G, group, 0)


def _gdn_layer(h, gn, w_in, conv_w, a_log, dt_bias, norm_g):
    B, S, D = h.shape
    T = SEQ_TILE
    n_vheads = a_log.shape[0]
    val_dim = n_vheads * GDN_HEAD_DIM
    key_dim = val_dim // 2
    n_qkv = 2 * key_dim + val_dim
    n_taps = conv_w.shape[0]
    assert w_in.shape[1] == n_qkv + val_dim + 2 * n_vheads and 3 * n_vheads <= LANES
    w_qkv = w_in[:, :n_qkv].astype(BF16)
    w_z = w_in[:, n_qkv:n_qkv + val_dim].astype(BF16)
    w_ba = jnp.pad(w_in[:, n_qkv + val_dim:], ((0, 0), (0, LANES - 2 * n_vheads))).astype(BF16)
    lane = jnp.arange(LANES)
    zeros_h = jnp.zeros((LANES - 2 * n_vheads,), F32)
    gate_tab = jnp.stack([
        (lane < n_vheads).astype(F32),
        ((lane >= n_vheads) & (lane < 2 * n_vheads)).astype(F32),
        jnp.concatenate([jnp.zeros((n_vheads,), F32), a_log.astype(F32), zeros_h]),
        jnp.concatenate([jnp.zeros((n_vheads,), F32), dt_bias.astype(F32), zeros_h]),
    ] + [jnp.zeros((LANES,), F32)] * 4)
    CW = 4 * LANES
    kern = functools.partial(_gdn_proj_kernel, n_taps=n_taps, key_dim=key_dim, n_vheads=n_vheads)
    q, k, v, z, bg = pl.pallas_call(
        kern,
        grid=(B, S // T),
        in_specs=[
            pl.BlockSpec((1, T, D), lambda b, s: (b, s, 0)),
            _const_spec((1, D)),
            _const_spec((D, n_qkv)),
            _const_spec((n_taps, n_qkv)),
            _const_spec((D, val_dim)),
            _const_spec((D, LANES)),
            _const_spec((SUBLANES, LANES)),
        ],
        out_specs=[
            pl.BlockSpec((1, T, key_dim), lambda b, s: (b, s, 0)),
            pl.BlockSpec((1, T, key_dim), lambda b, s: (b, s, 0)),
            pl.BlockSpec((1, T, val_dim), lambda b, s: (b, s, 0)),
            pl.BlockSpec((1, T, val_dim), lambda b, s: (b, s, 0)),
            pl.BlockSpec((1, T, LANES), lambda b, s: (b, s, 0)),
        ],
        out_shape=[
            jax.ShapeDtypeStruct((B, S, key_dim), BF16),
            jax.ShapeDtypeStruct((B, S, key_dim), BF16),
            jax.ShapeDtypeStruct((B, S, val_dim), BF16),
            jax.ShapeDtypeStruct((B, S, val_dim), BF16),
            jax.ShapeDtypeStruct((B, S, LANES), F32),
        ],
        scratch_shapes=[pltpu.VMEM((n_qkv // CW, SUBLANES, CW), F32),
                        pltpu.VMEM((2, SUBLANES + T, CW), F32)],
        compiler_params=_params("parallel", "arbitrary"),
        name="gdn_proj",
    )(h, gn.reshape(1, D), w_qkv, conv_w, w_z, w_ba, gate_tab)

    HB = GDN_HEADS_PER_STEP
    KB = HB // 2 * GDN_HEAD_DIM
    VB = HB * GDN_HEAD_DIM
    NG = n_vheads // HB
    assert n_vheads % HB == 0 and S % GDN_GROUP == 0
    kt = jnp.swapaxes(k, 1, 2)
    gate_cols = jnp.concatenate([bg[:, :, i * n_vheads:(i + 1) * n_vheads].reshape(B, S, NG, HB)
                                 for i in range(3)], axis=-1)
    gate_cols = gate_cols.transpose(0, 2, 1, 3)
    gate_rows = jnp.swapaxes(gate_cols, 2, 3)
    kern = functools.partial(_gdn_core_kernel, heads_per_step=HB)
    return pl.pallas_call(
        kern,
        grid=(B, NG),
        in_specs=[
            pl.BlockSpec((1, S, KB), lambda b, g: (b, 0, g)),
            pl.BlockSpec((1, S, KB), lambda b, g: (b, 0, g)),
            pl.BlockSpec((1, KB, S), lambda b, g: (b, g, 0)),
            pl.BlockSpec((1, S, VB), lambda b, g: (b, 0, g)),
            pl.BlockSpec((1, S, VB), lambda b, g: (b, 0, g)),
            pl.BlockSpec((1, 1, S, 3 * HB), lambda b, g: (b, g, 0, 0)),
            pl.BlockSpec((1, 1, 3 * HB, S), lambda b, g: (b, g, 0, 0)),
            _const_spec((1, GDN_HEAD_DIM)),
        ],
        out_specs=pl.BlockSpec((1, S, VB), lambda b, g: (b, 0, g)),
        out_shape=jax.ShapeDtypeStruct((B, S, val_dim), BF16),
        scratch_shapes=[pltpu.VMEM((HB, GDN_HEAD_DIM, GDN_HEAD_DIM), F32)],
        compiler_params=_params("parallel", "parallel"),
        name="gdn_core",
    )(q, k, kt, v, z, gate_cols, gate_rows, norm_g.reshape(1, GDN_HEAD_DIM))


def kernel(x, positions, norm_mix, norm_ffn, norm_final, attn_w_qkv, attn_lambda, attn_subln, attn_w_o,
           conv_w_in, conv_b_in, conv_dw, conv_dw_b, conv_ln_g, conv_ln_b, conv_w_out, conv_b_out,
           gdn_w_in, gdn_conv, gdn_a_log, gdn_dt_bias, gdn_norm, gdn_w_o, ffn_w_in, ffn_dw, ffn_dw_b,
           ffn_w_out):
    depth = norm_mix.shape[0]
    d_model = x.shape[-1]
    no_bias = jnp.zeros((d_model,), F32)
    h = x
    for i in range(depth):
        j = i // N_MIXERS
        kind = i % N_MIXERS
        if kind == 0:
            mix = _attention_layer(h, positions, norm_mix[i], attn_w_qkv[j], attn_lambda[j],
                                   attn_subln[j], i)
            w_o, b_o = attn_w_o[j], no_bias
        elif kind == 1:
            mix = _conformer_layer(h, norm_mix[i], conv_w_in[j], conv_b_in[j], conv_dw[j], conv_dw_b[j],
                                   conv_ln_g[j], conv_ln_b[j])
            w_o, b_o = conv_w_out[j], conv_b_out[j]
        else:
            mix = _gdn_layer(h, norm_mix[i], gdn_w_in[j], gdn_conv[j], gdn_a_log[j], gdn_dt_bias[j],
                             gdn_norm[j])
            w_o, b_o = gdn_w_o[j], no_bias
        h = _ffn_layer(h, mix, w_o, b_o, norm_ffn[i], ffn_w_in[i], ffn_dw[i], ffn_dw_b[i], ffn_w_out[i],
                       norm_final, final_norm=(i == depth - 1))
    return h
```
